```python
import jax, jax.numpy as jnp
from jax import lax
import numpy as np

D_MODEL = 2048
BATCH = 2
SEQ = 4096
DEPTH = 4

N_MIXERS = 3
EPS = 1e-6
SB_HEADS = 16
SB_HEAD_DIM = D_MODEL // SB_HEADS
SB_WIDTH = SB_HEADS * SB_HEAD_DIM
Q_BLOCK = 128
GM_CHUNK = 128
GM_GROUPS = 16
GM_WIDTH = D_MODEL
GM_GROUP_DIM = GM_WIDTH // GM_GROUPS
SSD_INNER = 2 * D_MODEL
SSD_HEAD_DIM = 64
SSD_HEADS = SSD_INNER // SSD_HEAD_DIM
SSD_GROUPS = 8
SSD_HPG = SSD_HEADS // SSD_GROUPS
SSD_STATE = 128
SSD_CONV = 4
SSD_CHUNK = 128
SSD_CONV_DIM = SSD_INNER + 2 * SSD_GROUPS * SSD_STATE
SSD_PROJ = SSD_INNER + SSD_CONV_DIM + SSD_HEADS
MLP_HIDDEN = 4 * D_MODEL
N_A = len(range(0, DEPTH, N_MIXERS))
N_B = len(range(1, DEPTH, N_MIXERS))
N_C = len(range(2, DEPTH, N_MIXERS))

kernel_name = "hybrid_sb_gmlp_ssd_trunk"


def rms_norm(x, g):
    xf = x.astype(jnp.float32)
    y = xf * lax.rsqrt(jnp.mean(xf * xf, axis=-1, keepdims=True) + EPS)
    return (y * g.astype(jnp.float32)).astype(x.dtype)


def stick_breaking_attention(h, w_qkv, q_g, k_g, w_o):
    b, s, _ = h.shape
    q, k, v = jnp.split(h @ w_qkv, 3, axis=-1)
    q = rms_norm(q.reshape(b, s, SB_HEADS, SB_HEAD_DIM), q_g).transpose(0, 2, 1, 3)
    k = rms_norm(k.reshape(b, s, SB_HEADS, SB_HEAD_DIM), k_g).transpose(0, 2, 1, 3)
    v = v.reshape(b, s, SB_HEADS, SB_HEAD_DIM).transpose(0, 2, 1, 3)
    scale = SB_HEAD_DIM ** -0.5
    n_blk = s // Q_BLOCK
    q_blocks = q.reshape(b, SB_HEADS, n_blk, Q_BLOCK, SB_HEAD_DIM).transpose(2, 0, 1, 3, 4)
    k_pos = jnp.arange(s)

    def one_block(args):
        qb, start = args
        z = jnp.einsum('bhqd,bhkd->bhqk', qb, k).astype(jnp.float32) * scale
        q_pos = start + jnp.arange(Q_BLOCK)
        mask = k_pos[None, :] < q_pos[:, None]
        log_beta = jax.nn.log_sigmoid(z)
        log_1m = jnp.where(mask, jax.nn.log_sigmoid(-z), 0.0)
        suffix = lax.cumsum(log_1m, axis=3, reverse=True) - log_1m
        a = jnp.where(mask, jnp.exp(log_beta + suffix), 0.0)
        return jnp.einsum('bhqk,bhkd->bhqd', a.astype(v.dtype), v)

    starts = jnp.arange(n_blk, dtype=jnp.int32) * Q_BLOCK
    o = lax.map(one_block, (q_blocks, starts))
    o = o.transpose(1, 0, 3, 2, 4).reshape(b, s, SB_WIDTH)
    return o @ w_o


def chunked_spatial_gating(h, w_in, v_g, w_s, b_s, w_o):
    b, s, _ = h.shape
    u, v = jnp.split(jax.nn.gelu(h @ w_in, approximate=False), 2, axis=-1)
    v = rms_norm(v, v_g)
    n_chunk = s // GM_CHUNK
    v = v.reshape(b, n_chunk, GM_CHUNK, GM_GROUPS, GM_GROUP_DIM)
    u = u.reshape(b, n_chunk, GM_CHUNK, GM_GROUPS, GM_GROUP_DIM)
    causal = jnp.tril(jnp.ones((GM_CHUNK, GM_CHUNK), dtype=bool))
    w = jnp.where(causal[None], w_s, 0.0)
    mixed = jnp.einsum('gts,bcsgd->bctgd', w.astype(v.dtype), v) + b_s.T[:, :, None]
    y = (u * mixed).reshape(b, s, GM_WIDTH)
    return y @ w_o


def ssd_chunked_scan(x, dt, a, bm, cm):
    b, s = x.shape[:2]
    c, L = s // SSD_CHUNK, SSD_CHUNK
    xs = (x * dt[..., None]).reshape(b, c, L, SSD_GROUPS, SSD_HPG, SSD_HEAD_DIM)
    da = (dt * a).reshape(b, c, L, SSD_GROUPS, SSD_HPG).transpose(0, 3, 4, 1, 2)
    bc = bm.reshape(b, c, L, SSD_GROUPS, SSD_STATE)
    cc = cm.reshape(b, c, L, SSD_GROUPS, SSD_STATE)
    a_cum = jnp.cumsum(da, axis=-1)
    seg = a_cum[..., :, None] - a_cum[..., None, :]
    tri = jnp.tril(jnp.ones((L, L), dtype=bool))
    decay = jnp.exp(jnp.where(tri, seg, -jnp.inf))
    cb = jnp.einsum('bclgn,bcsgn->bgcls', cc, bc)
    y_diag = jnp.einsum('bgcls,bgrcls,bcsgrp->bclgrp', cb, decay, xs)
    decay_states = jnp.exp(a_cum[..., -1:] - a_cum)
    states = jnp.einsum('bclgn,bgrcl,bclgrp->cbgrpn', bc, decay_states, xs)
    chunk_decay = jnp.exp(a_cum[..., -1]).transpose(3, 0, 1, 2)

    def step(carry, inp):
        st, dec = inp
        return carry * dec[..., None, None] + st, carry

    _, prev = lax.scan(step, jnp.zeros_like(states[0]), (states, chunk_decay))
    y_off = jnp.einsum('bclgn,cbgrpn,bgrcl->bclgrp', cc, prev, jnp.exp(a_cum))
    return (y_diag + y_off).reshape(b, s, SSD_GROUPS, SSD_HPG, SSD_HEAD_DIM)


def ssd_mixer(h, w_in, conv_w, conv_b, dt_bias, a_log, d_skip, norm_g, w_o):
    b, s, _ = h.shape
    z, xbc, dt = jnp.split(h @ w_in, [SSD_INNER, SSD_INNER + SSD_CONV_DIM], axis=-1)
    xbc = lax.conv_general_dilated(xbc, conv_w[:, None, :].astype(xbc.dtype), window_strides=(1,),
                                   padding=[(SSD_CONV - 1, 0)],
                                   dimension_numbers=('NWC', 'WIO', 'NWC'),
                                   feature_group_count=SSD_CONV_DIM) + conv_b
    xbc = jax.nn.silu(xbc)
    xi, bm, cm = jnp.split(xbc, [SSD_INNER, SSD_INNER + SSD_GROUPS * SSD_STATE], axis=-1)
    f32 = jnp.float32
    dt = jax.nn.softplus(dt.astype(f32) + dt_bias.astype(f32))
    a = -jnp.exp(a_log.astype(f32))
    xi = xi.astype(f32).reshape(b, s, SSD_GROUPS, SSD_HPG, SSD_HEAD_DIM)
    y = ssd_chunked_scan(xi, dt.reshape(b, s, SSD_GROUPS, SSD_HPG), a.reshape(SSD_GROUPS, SSD_HPG),
                         bm.astype(f32).reshape(b, s, SSD_GROUPS, SSD_STATE),
                         cm.astype(f32).reshape(b, s, SSD_GROUPS, SSD_STATE))
    y = y + d_skip.astype(f32).reshape(SSD_GROUPS, SSD_HPG)[:, :, None] * xi
    y = y.reshape(b, s, SSD_INNER).astype(h.dtype) * jax.nn.silu(z)
    y = rms_norm(y.reshape(b, s, SSD_GROUPS, SSD_INNER // SSD_GROUPS),
                 norm_g.reshape(SSD_GROUPS, SSD_INNER // SSD_GROUPS)).reshape(b, s, SSD_INNER)
    return y @ w_o


def squared_relu_mlp(h, w_in, w_out):
    return jnp.square(jax.nn.relu(h @ w_in)) @ w_out


def setup_inputs(seed: int = 0) -> dict:
    key = jax.random.key(seed)
    ks = jax.random.split(key, 24)
    nrm = jax.random.normal
    f32 = jnp.float32
    d = D_MODEL
    dt = jnp.exp(jax.random.uniform(ks[14], (N_C, SSD_HEADS), f32, np.log(1e-3), np.log(1e-1)))
    return {
        "x": nrm(ks[0], (BATCH, SEQ, d), f32),
        "norm_mix_g": 1.0 + 0.02 * nrm(ks[1], (DEPTH, d), f32),
        "norm_mlp_g": 1.0 + 0.02 * nrm(ks[2], (DEPTH, d), f32),
        "sb_w_qkv": nrm(ks[3], (N_A, d, 3 * SB_WIDTH), f32) * d ** -0.5,
        "sb_q_norm_g": 1.0 + 0.02 * nrm(ks[4], (N_A, SB_HEAD_DIM), f32),
        "sb_k_norm_g": 1.0 + 0.02 * nrm(ks[5], (N_A, SB_HEAD_DIM), f32),
        "sb_w_o": nrm(ks[6], (N_A, SB_WIDTH, d), f32) * SB_WIDTH ** -0.5,
        "gm_w_in": nrm(ks[7], (N_B, d, 2 * GM_WIDTH), f32) * d ** -0.5,
        "gm_v_norm_g": 1.0 + 0.02 * nrm(ks[8], (N_B, GM_WIDTH), f32),
        "gm_w_s": nrm(ks[9], (N_B, GM_GROUPS, GM_CHUNK, GM_CHUNK), f32) * (1.0 / GM_CHUNK),
        "gm_b_s": 1.0 + 0.02 * nrm(ks[10], (N_B, GM_GROUPS, GM_CHUNK), f32),
        "gm_w_o": nrm(ks[11], (N_B, GM_WIDTH, d), f32) * GM_WIDTH ** -0.5,
        "ssd_w_in": nrm(ks[12], (N_C, d, SSD_PROJ), f32) * d ** -0.5,
        "ssd_conv_w": nrm(ks[13], (N_C, SSD_CONV, SSD_CONV_DIM), f32) * SSD_CONV ** -0.5,
        "ssd_conv_b": 0.02 * nrm(ks[15], (N_C, SSD_CONV_DIM), f32),
        "ssd_dt_bias": dt + jnp.log(-jnp.expm1(-dt)),
        "ssd_a_log": jnp.log(jax.random.uniform(ks[16], (N_C, SSD_HEADS), f32, 1.0, 16.0)),
        "ssd_d": 1.0 + 0.02 * nrm(ks[17], (N_C, SSD_HEADS), f32),
        "ssd_norm_g": 1.0 + 0.02 * nrm(ks[18], (N_C, SSD_INNER), f32),
        "ssd_w_o": nrm(ks[19], (N_C, SSD_INNER, d), f32) * SSD_INNER ** -0.5,
        "mlp_w_in": nrm(ks[20], (DEPTH, d, MLP_HIDDEN), f32) * d ** -0.5,
        "mlp_w_out": nrm(ks[21], (DEPTH, MLP_HIDDEN, d), f32) * MLP_HIDDEN ** -0.5,
    }


def reference(x, norm_mix_g, norm_mlp_g, sb_w_qkv, sb_q_norm_g, sb_k_norm_g, sb_w_o,
              gm_w_in, gm_v_norm_g, gm_w_s, gm_b_s, gm_w_o,
              ssd_w_in, ssd_conv_w, ssd_conv_b, ssd_dt_bias, ssd_a_log, ssd_d, ssd_norm_g, ssd_w_o,
              mlp_w_in, mlp_w_out):
    h = x
    for i in range(DEPTH):
        kind, j = i % N_MIXERS, i // N_MIXERS
        hn = rms_norm(h, norm_mix_g[i])
        if kind == 0:
            mix = stick_breaking_attention(hn, sb_w_qkv[j], sb_q_norm_g[j], sb_k_norm_g[j], sb_w_o[j])
        elif kind == 1:
            mix = chunked_spatial_gating(hn, gm_w_in[j], gm_v_norm_g[j], gm_w_s[j], gm_b_s[j], gm_w_o[j])
        else:
            mix = ssd_mixer(hn, ssd_w_in[j], ssd_conv_w[j], ssd_conv_b[j], ssd_dt_bias[j],
                            ssd_a_log[j], ssd_d[j], ssd_norm_g[j], ssd_w_o[j])
        h = h + mix
        h = h + squared_relu_mlp(rms_norm(h, norm_mlp_g[i]), mlp_w_in[i], mlp_w_out[i])
    return h
```

```python
import functools

import jax
import jax.numpy as jnp
import numpy as np
from jax import lax
from jax.experimental import pallas as pl
from jax.experimental.pallas import tpu as pltpu

F32 = jnp.float32
BF16 = jnp.bfloat16

EPS = 1e-6
LANE = 128
SUBLANE = 8
N_MIXERS = 3

SB_HEAD_DIM = 128
GM_CHUNK = 128
GM_GROUP_DIM = 128
SSD_HEAD_DIM = 64
SSD_GROUPS = 8
SSD_STATE = 128
SSD_CONV = 4
SSD_CHUNK = 128

VMEM_LIMIT_BYTES = 48 * 1024 * 1024


def _cparams(*sem):
    return pltpu.CompilerParams(dimension_semantics=sem, vmem_limit_bytes=VMEM_LIMIT_BYTES)


def _rms(x, g):
    ms = jnp.mean(x * x, axis=-1, keepdims=True)
    return x * lax.rsqrt(ms + EPS) * g


def _softplus(x):
    return jnp.maximum(x, 0.0) + jnp.log1p(jnp.exp(-jnp.abs(x)))


def _silu(x):
    return x * jax.nn.sigmoid(x)


def _split3(x):
    hi = x.astype(BF16)
    r = x - hi.astype(F32)
    mid = r.astype(BF16)
    lo = (r - mid.astype(F32)).astype(BF16)
    return hi, mid, lo


def _dot(a, b):
    return jnp.dot(a, b, preferred_element_type=F32)


def _dot_nt(a, b):
    return lax.dot_general(a, b, (((1,), (1,)), ((), ())), preferred_element_type=F32)


def _dot3_lhs(x, m):
    hi, mid, lo = _split3(x)
    return _dot(hi, m) + _dot(mid, m) + _dot(lo, m)


def _dot3_rhs(m, x):
    hi, mid, lo = _split3(x)
    return _dot(m, hi) + _dot(m, mid) + _dot(m, lo)


def _norm_mm_kernel(x_ref, g_ref, w_ref, o_ref, xn_ref, *, act):
    @pl.when(pl.program_id(1) == 0)
    def _():
        xn_ref[...] = _rms(x_ref[...], g_ref[...]).astype(BF16)

    acc = _dot(xn_ref[...], w_ref[...])
    if act == "gelu":
        acc = 0.5 * acc * (1.0 + lax.erf(acc * np.float32(np.sqrt(0.5))))
    o_ref[...] = acc.astype(o_ref.dtype)


def _norm_matmul(x, g, w, *, out_dtype, act=None, tm, tn):
    t, d = x.shape
    n = w.shape[1]
    return pl.pallas_call(
        functools.partial(_norm_mm_kernel, act=act),
        grid=(t // tm, n // tn),
        in_specs=[
            pl.BlockSpec((tm, d), lambda i, j: (i, 0)),
            pl.BlockSpec((1, d), lambda i, j: (0, 0)),
            pl.BlockSpec((d, tn), lambda i, j: (0, j)),
        ],
        out_specs=pl.BlockSpec((tm, tn), lambda i, j: (i, j)),
        out_shape=jax.ShapeDtypeStruct((t, n), out_dtype),
        scratch_shapes=[pltpu.VMEM((tm, d), BF16)],
        compiler_params=_cparams("parallel", "arbitrary"),
        name="norm_matmul",
    )(x, g.reshape(1, d), w)


def _qkv_kernel(x_ref, g_ref, w_ref, qg_ref, kg_ref, o_ref, xn_ref, *, tiles_per_part):
    j = pl.program_id(1)

    @pl.when(j == 0)
    def _():
        xn_ref[...] = _rms(x_ref[...], g_ref[...]).astype(BF16)

    acc = _dot(xn_ref[...], w_ref[...])
    heads_per_tile = acc.shape[1] // SB_HEAD_DIM

    def head_norm(gain):
        for c in range(heads_per_tile):
            cols = slice(c * SB_HEAD_DIM, (c + 1) * SB_HEAD_DIM)
            o_ref[:, cols] = _rms(acc[:, cols], gain).astype(o_ref.dtype)

    @pl.when(j < tiles_per_part)
    def _():
        head_norm(qg_ref[...])

    @pl.when(jnp.logical_and(j >= tiles_per_part, j < 2 * tiles_per_part))
    def _():
        head_norm(kg_ref[...])

    @pl.when(j >= 2 * tiles_per_part)
    def _():
        o_ref[...] = acc.astype(o_ref.dtype)


def _qkv_proj(x, g, w, q_g, k_g, *, tm, tn):
    t, d = x.shape
    n = w.shape[1]
    width = n // 3
    return pl.pallas_call(
        functools.partial(_qkv_kernel, tiles_per_part=width // tn),
        grid=(t // tm, n // tn),
        in_specs=[
            pl.BlockSpec((tm, d), lambda i, j: (i, 0)),
            pl.BlockSpec((1, d), lambda i, j: (0, 0)),
            pl.BlockSpec((d, tn), lambda i, j: (0, j)),
            pl.BlockSpec((1, SB_HEAD_DIM), lambda i, j: (0, 0)),
            pl.BlockSpec((1, SB_HEAD_DIM), lambda i, j: (0, 0)),
        ],
        out_specs=pl.BlockSpec((tm, tn), lambda i, j: (i, j)),
        out_shape=jax.ShapeDtypeStruct((t, n), BF16),
        scratch_shapes=[pltpu.VMEM((tm, d), BF16)],
        compiler_params=_cparams("parallel", "arbitrary"),
        name="qkv_proj",
    )(x, g.reshape(1, d), w, q_g.reshape(1, -1), k_g.reshape(1, -1))


def _sb_attn_kernel(q_ref, k_ref, v_ref, u_ref, o_ref, *, tq, scale):
    qi = pl.program_id(2)
    q = q_ref[...]
    upper = u_ref[...]

    def tile(j, carry, acc, diagonal):
        start = pl.multiple_of(j * tq, tq)
        kb = k_ref[pl.ds(start, tq), :]
        vb = v_ref[pl.ds(start, tq), :]
        z = _dot_nt(q, kb) * scale
        lse = jnp.log1p(jnp.exp(-jnp.abs(z)))
        log_beta = jnp.minimum(z, 0.0) - lse
        log_1m = log_beta - z
        if diagonal:
            row = lax.broadcasted_iota(jnp.int32, (tq, tq), 0)
            col = lax.broadcasted_iota(jnp.int32, (tq, tq), 1)
            causal = col < row
            log_1m = jnp.where(causal, log_1m, 0.0)
        hi = log_1m.astype(BF16)
        lo = (log_1m - hi.astype(F32)).astype(BF16)
        suffix = _dot(hi, upper) + _dot(lo, upper)
        a = jnp.exp(log_beta + suffix + carry)
        if diagonal:
            a = jnp.where(causal, a, 0.0)
        acc = acc + _dot(a.astype(BF16), vb)
        carry = carry + jnp.sum(log_1m, axis=-1, keepdims=True)
        return carry, acc

    carry = jnp.zeros((tq, 1), F32)
    acc = jnp.zeros((tq, SB_HEAD_DIM), F32)
    carry, acc = tile(qi, carry, acc, True)

    def body(step, c):
        return tile(qi - 1 - step, c[0], c[1], False)

    carry, acc = lax.fori_loop(0, qi, body, (carry, acc))
    o_ref[...] = acc.astype(o_ref.dtype)


def _sb_attention(qkv, batch, seq, *, tq):
    t, n = qkv.shape
    heads = n // (3 * SB_HEAD_DIM)
    nq = seq // tq
    s_idx = np.arange(tq)[:, None]
    j_idx = np.arange(tq)[None, :]
    upper = jnp.asarray((s_idx > j_idx).astype(np.float32), dtype=BF16)
    return pl.pallas_call(
        functools.partial(_sb_attn_kernel, tq=tq, scale=np.float32(SB_HEAD_DIM ** -0.5)),
        grid=(batch, heads, nq),
        in_specs=[
            pl.BlockSpec((tq, SB_HEAD_DIM), lambda b, h, i: (b * nq + i, h)),
            pl.BlockSpec((seq, SB_HEAD_DIM), lambda b, h, i: (b, heads + h)),
            pl.BlockSpec((seq, SB_HEAD_DIM), lambda b, h, i: (b, 2 * heads + h)),
            pl.BlockSpec((tq, tq), lambda b, h, i: (0, 0)),
        ],
        out_specs=pl.BlockSpec((tq, SB_HEAD_DIM), lambda b, h, i: (b * nq + i, h)),
        out_shape=jax.ShapeDtypeStruct((t, heads * SB_HEAD_DIM), BF16),
        compiler_params=_cparams("parallel", "parallel", "arbitrary"),
        name="sb_attention",
    )(qkv, qkv, qkv, upper)


def _mm_res_kernel(a_ref, w_ref, r_ref, o_ref):
    o_ref[...] = r_ref[...] + _dot(a_ref[...], w_ref[...])


def _matmul_residual(a, w, res, *, tm, tn):
    t, k = a.shape
    n = w.shape[1]
    return pl.pallas_call(
        _mm_res_kernel,
        grid=(t // tm, n // tn),
        in_specs=[
            pl.BlockSpec((tm, k), lambda i, j: (i, 0)),
            pl.BlockSpec((k, tn), lambda i, j: (0, j)),
            pl.BlockSpec((tm, tn), lambda i, j: (i, j)),
        ],
        out_specs=pl.BlockSpec((tm, tn), lambda i, j: (i, j)),
        out_shape=jax.ShapeDtypeStruct((t, n), F32),
        compiler_params=_cparams("parallel", "arbitrary"),
        name="matmul_residual",
    )(a, w, res)


def _mlp_kernel(x_ref, g_ref, w1_ref, w2_ref, o_ref, xn_ref):
    @pl.when(pl.program_id(1) == 0)
    def _():
        x = x_ref[...]
        xn_ref[...] = _rms(x, g_ref[...]).astype(BF16)
        o_ref[...] = x

    h = _dot(xn_ref[...], w1_ref[...])
    h = jnp.square(jnp.maximum(h, 0.0))
    o_ref[...] += _dot(h.astype(BF16), w2_ref[...])


def _mlp(x, g, w1, w2, *, tm, th):
    t, d = x.shape
    hidden = w1.shape[1]
    return pl.pallas_call(
        _mlp_kernel,
        grid=(t // tm, hidden // th),
        in_specs=[
            pl.BlockSpec((tm, d), lambda i, k: (i, 0)),
            pl.BlockSpec((1, d), lambda i, k: (0, 0)),
            pl.BlockSpec((d, th), lambda i, k: (0, k)),
            pl.BlockSpec((th, d), lambda i, k: (k, 0)),
        ],
        out_specs=pl.BlockSpec((tm, d), lambda i, k: (i, 0)),
        out_shape=jax.ShapeDtypeStruct((t, d), F32),
        scratch_shapes=[pltpu.VMEM((tm, d), BF16)],
        compiler_params=_cparams("parallel", "arbitrary"),
        name="relu2_mlp",
    )(x, g.reshape(1, d), w1, w2)


def _gm_gate_kernel(u_ref, v_ref, vg_ref, ws_ref, bst_ref, x_ref, wo_ref, o_ref, vn_ref, y_ref):
    tm = u_ref.shape[0]
    groups = ws_ref.shape[0]
    vn_ref[...] = _rms(v_ref[...].astype(F32), vg_ref[...]).astype(BF16)
    row = lax.broadcasted_iota(jnp.int32, (GM_CHUNK, GM_CHUNK), 0)
    col = lax.broadcasted_iota(jnp.int32, (GM_CHUNK, GM_CHUNK), 1)
    causal = col <= row
    for g in range(groups):
        cols = slice(g * GM_GROUP_DIM, (g + 1) * GM_GROUP_DIM)
        w = jnp.where(causal, ws_ref[g], 0.0).astype(BF16)
        bias = bst_ref[:, g:g + 1]
        for c in range(tm // GM_CHUNK):
            rows = slice(c * GM_CHUNK, (c + 1) * GM_CHUNK)
            mixed = _dot(w, vn_ref[rows, cols]) + bias
            y_ref[rows, cols] = (u_ref[rows, cols].astype(F32) * mixed).astype(BF16)
    o_ref[...] = x_ref[...] + _dot(y_ref[...], wo_ref[...])


def _gm_gate(uv, v_g, w_s, b_s, x, w_o, *, tm):
    t, d = x.shape
    width = uv.shape[1] // 2
    groups = w_s.shape[0]
    return pl.pallas_call(
        _gm_gate_kernel,
        grid=(t // tm,),
        in_specs=[
            pl.BlockSpec((tm, width), lambda i: (i, 0)),
            pl.BlockSpec((tm, width), lambda i: (i, 1)),
            pl.BlockSpec((1, width), lambda i: (0, 0)),
            pl.BlockSpec((groups, GM_CHUNK, GM_CHUNK), lambda i: (0, 0, 0)),
            pl.BlockSpec((GM_CHUNK, groups), lambda i: (0, 0)),
            pl.BlockSpec((tm, d), lambda i: (i, 0)),
            pl.BlockSpec((width, d), lambda i: (0, 0)),
        ],
        out_specs=pl.BlockSpec((tm, d), lambda i: (i, 0)),
        out_shape=jax.ShapeDtypeStruct((t, d), F32),
        scratch_shapes=[pltpu.VMEM((tm, width), BF16), pltpu.VMEM((tm, width), BF16)],
        compiler_params=_cparams("parallel"),
        name="gmlp_gate",
    )(uv, uv, v_g.reshape(1, width), w_s, b_s.T, x, w_o)


def _causal_conv_silu(x, tail, w, b):
    rows = x.shape[0]
    acc = x * w[SSD_CONV - 1:SSD_CONV, :]
    top = x[0:SUBLANE] * w[SSD_CONV - 1:SSD_CONV, :]
    row8 = lax.broadcasted_iota(jnp.int32, (SUBLANE, x.shape[1]), 0)
    for s in range(1, SSD_CONV):
        wk = w[SSD_CONV - 1 - s:SSD_CONV - s, :]
        rolled = pltpu.roll(x, s, 0)
        acc = acc + rolled * wk
        prev = pltpu.roll(tail, s, 0)
        top = top + jnp.where(row8 < s, prev, rolled[0:SUBLANE]) * wk
    y = jnp.concatenate([top, acc[SUBLANE:rows]], axis=0) + b
    return _silu(y)


def _ssd_kernel(z_ref, xi_ref, bm_ref, cm_ref, dt_ref,
                wx_ref, wb_ref, wc_ref, bx_ref, bb_ref, bc_ref,
                dtb_ref, alog_ref, e_ref, dskip_ref, ng_ref, tri_ref,
                o_ref,
                state_ref, tx_ref, tb_ref, tc_ref, dts_ref, ac_ref, act_ref):
    c = pl.program_id(1)
    g = pl.program_id(2)
    L = SSD_CHUNK
    hpg = e_ref.shape[1] // SSD_HEAD_DIM

    @pl.when(c == 0)
    def _():
        state_ref[g] = jnp.zeros(state_ref.shape[1:], F32)
        tx_ref[g] = jnp.zeros(tx_ref.shape[1:], F32)
        tb_ref[g] = jnp.zeros(tb_ref.shape[1:], F32)
        tc_ref[g] = jnp.zeros(tc_ref.shape[1:], F32)

    @pl.when(g == 0)
    def _():
        dt = _softplus(dt_ref[...] + dtb_ref[...])
        a = -jnp.exp(alog_ref[...])
        a_cum = _dot3_rhs(tri_ref[...], dt * a)
        dts_ref[...] = dt
        ac_ref[...] = a_cum
        act_ref[...] = a_cum.T

    x_raw = xi_ref[...]
    b_raw = bm_ref[...]
    c_raw = cm_ref[...]
    xi = _causal_conv_silu(x_raw, tx_ref[g], wx_ref[...], bx_ref[...])
    bm = _causal_conv_silu(b_raw, tb_ref[g], wb_ref[...], bb_ref[...])
    cm = _causal_conv_silu(c_raw, tc_ref[g], wc_ref[...], bc_ref[...])
    tx_ref[g] = x_raw[L - SUBLANE:L]
    tb_ref[g] = b_raw[L - SUBLANE:L]
    tc_ref[g] = c_raw[L - SUBLANE:L]

    expand = e_ref[...]
    a_cum = ac_ref[...]
    dt_x = _dot3_lhs(dts_ref[...], expand)
    ac_x = _dot3_lhs(a_cum, expand)
    last_x = ac_x[L - 1:L, :]
    xs = xi * dt_x
    xsd = xs * jnp.exp(last_x - ac_x)

    cm_b = cm.astype(BF16)
    bm_b = bm.astype(BF16)
    xs_b = xs.astype(BF16)
    state = state_ref[g]
    y = _dot(cm_b, state.astype(BF16)) * jnp.exp(ac_x)
    state_ref[g] = jnp.exp(last_x) * state + _dot(bm.T.astype(BF16), xsd.astype(BF16))

    cb = _dot_nt(cm_b, bm_b)
    row = lax.broadcasted_iota(jnp.int32, (L, L), 0)
    col = lax.broadcasted_iota(jnp.int32, (L, L), 1)
    tri = col <= row
    lane = lax.broadcasted_iota(jnp.int32, (L, 2 * SSD_HEAD_DIM), 1)
    first = lane < SSD_HEAD_DIM
    y_diag = []
    for pair in range(hpg // 2):
        outs = []
        for r in (2 * pair, 2 * pair + 1):
            a_col = ac_x[:, r * SSD_HEAD_DIM:r * SSD_HEAD_DIM + 1]
            a_row = act_ref[pl.ds(g * hpg + r, 1), :]
            decay = jnp.exp(jnp.where(tri, a_col - a_row, -jnp.inf))
            m = (cb * decay).astype(BF16)
            outs.append(_dot(m, xs_b[:, pair * 2 * SSD_HEAD_DIM:(pair + 1) * 2 * SSD_HEAD_DIM]))
        y_diag.append(jnp.where(first, outs[0], outs[1]))
    y = y + jnp.concatenate(y_diag, axis=1)
    y = y + dskip_ref[...] * xi
    y = y * _silu(z_ref[...])
    o_ref[...] = _rms(y, ng_ref[...]).astype(o_ref.dtype)


def _ssd_scan(zxbc, dt_raw, conv_w, conv_b, dt_bias, a_log, d_skip, norm_g, batch, seq):
    t = zxbc.shape[0]
    heads = d_skip.shape[0]
    inner = heads * SSD_HEAD_DIM
    gw = inner // SSD_GROUPS
    hpg = heads // SSD_GROUPS
    nc = seq // SSD_CHUNK
    hp = dt_raw.shape[1]
    L = SSD_CHUNK

    pad = hp - heads
    dtb = jnp.pad(dt_bias, (0, pad)).reshape(1, hp)
    alog = jnp.pad(a_log, (0, pad)).reshape(1, hp)
    e = np.zeros((hp, inner), np.float32)
    for h in range(heads):
        e[h, h * SSD_HEAD_DIM:(h + 1) * SSD_HEAD_DIM] = 1.0
    e = jnp.asarray(e, dtype=BF16)
    tri = jnp.asarray(np.tril(np.ones((L, L), np.float32)), dtype=BF16)
    dskip_x = jnp.repeat(d_skip, SSD_HEAD_DIM).reshape(1, inner)
    conv_b2 = conv_b.reshape(1, -1)

    zb = inner // gw
    xb = 0
    nb = inner // SSD_STATE
    row_blk = lambda b, c, g: b * nc + c
    in_specs = [
        pl.BlockSpec((L, gw), lambda b, c, g: (row_blk(b, c, g), g)),
        pl.BlockSpec((L, gw), lambda b, c, g: (row_blk(b, c, g), zb + g)),
        pl.BlockSpec((L, SSD_STATE), lambda b, c, g: (row_blk(b, c, g), 2 * nb + g)),
        pl.BlockSpec((L, SSD_STATE), lambda b, c, g: (row_blk(b, c, g), 2 * nb + SSD_GROUPS + g)),
        pl.BlockSpec((L, hp), lambda b, c, g: (row_blk(b, c, g), 0)),
        pl.BlockSpec((SSD_CONV, gw), lambda b, c, g: (0, xb + g)),
        pl.BlockSpec((SSD_CONV, SSD_STATE), lambda b, c, g: (0, nb + g)),
        pl.BlockSpec((SSD_CONV, SSD_STATE), lambda b, c, g: (0, nb + SSD_GROUPS + g)),
        pl.BlockSpec((1, gw), lambda b, c, g: (0, xb + g)),
        pl.BlockSpec((1, SSD_STATE), lambda b, c, g: (0, nb + g)),
        pl.BlockSpec((1, SSD_STATE), lambda b, c, g: (0, nb + SSD_GROUPS + g)),
        pl.BlockSpec((1, hp), lambda b, c, g: (0, 0)),
        pl.BlockSpec((1, hp), lambda b, c, g: (0, 0)),
        pl.BlockSpec((hp, gw), lambda b, c, g: (0, g)),
        pl.BlockSpec((1, gw), lambda b, c, g: (0, g)),
        pl.BlockSpec((1, gw), lambda b, c, g: (0, g)),
        pl.BlockSpec((L, L), lambda b, c, g: (0, 0)),
    ]
    return pl.pallas_call(
        _ssd_kernel,
        grid=(batch, nc, SSD_GROUPS),
        in_specs=in_specs,
        out_specs=pl.BlockSpec((L, gw), lambda b, c, g: (row_blk(b, c, g), g)),
        out_shape=jax.ShapeDtypeStruct((t, inner), BF16),
        scratch_shapes=[
            pltpu.VMEM((SSD_GROUPS, SSD_STATE, gw), F32),
            pltpu.VMEM((SSD_GROUPS, SUBLANE, gw), F32),
            pltpu.VMEM((SSD_GROUPS, SUBLANE, SSD_STATE), F32),
            pltpu.VMEM((SSD_GROUPS, SUBLANE, SSD_STATE), F32),
            pltpu.VMEM((L, hp), F32),
            pltpu.VMEM((L, hp), F32),
            pltpu.VMEM((hp, L), F32),
        ],
        compiler_params=_cparams("arbitrary", "arbitrary", "arbitrary"),
        name="ssd_scan",
    )(zxbc, zxbc, zxbc, zxbc, dt_raw,
      conv_w, conv_w, conv_w, conv_b2, conv_b2, conv_b2,
      dtb, alog, e, dskip_x, norm_g.reshape(1, inner), tri)


def _sb_layer(h, norm_g, w_qkv, q_g, k_g, w_o, batch, seq):
    qkv = _qkv_proj(h, norm_g, w_qkv.astype(BF16), q_g, k_g, tm=1024, tn=512)
    o = _sb_attention(qkv, batch, seq, tq=256)
    return _matmul_residual(o, w_o.astype(BF16), h, tm=1024, tn=512)


def _gm_layer(h, norm_g, w_in, v_g, w_s, b_s, w_o):
    uv = _norm_matmul(h, norm_g, w_in.astype(BF16), out_dtype=BF16, act="gelu", tm=1024, tn=512)
    return _gm_gate(uv, v_g, w_s, b_s, h, w_o.astype(BF16), tm=256)


def _ssd_layer(h, norm_g, w_in, conv_w, conv_b, dt_bias, a_log, d_skip, ssd_norm_g, w_o, batch, seq):
    heads = d_skip.shape[0]
    main = w_in.shape[1] - heads
    hp = -(-heads // LANE) * LANE
    w_main = w_in[:, :main].astype(BF16)
    w_dt = jnp.pad(w_in[:, main:], ((0, 0), (0, hp - heads))).astype(BF16)
    zxbc = _norm_matmul(h, norm_g, w_main, out_dtype=F32, tm=1024, tn=512)
    dt_raw = _norm_matmul(h, norm_g, w_dt, out_dtype=F32, tm=1024, tn=hp)
    y = _ssd_scan(zxbc, dt_raw, conv_w, conv_b, dt_bias, a_log, d_skip, ssd_norm_g, batch, seq)
    return _matmul_residual(y, w_o.astype(BF16), h, tm=1024, tn=512)


def kernel(x, norm_mix_g, norm_mlp_g, sb_w_qkv, sb_q_norm_g, sb_k_norm_g, sb_w_o, gm_w_in, gm_v_norm_g, gm_w_s, gm_b_s, gm_w_o, ssd_w_in, ssd_conv_w, ssd_conv_b, ssd_dt_bias, ssd_a_log, ssd_d, ssd_norm_g, ssd_w_o, mlp_w_in, mlp_w_out):
    batch, seq, d = x.shape
    depth = norm_mix_g.shape[0]
    h = x.reshape(batch * seq, d)
    for i in range(depth):
        kind, j = i % N_MIXERS, i // N_MIXERS
        if kind == 0:
            h = _sb_layer(h, norm_mix_g[i], sb_w_qkv[j], sb_q_norm_g[j], sb_k_norm_g[j], sb_w_o[j], batch, seq)
        elif kind == 1:
            h = _gm_layer(h, norm_mix_g[i], gm_w_in[j], gm_v_norm_g[j], gm_w_s[j], gm_b_s[j], gm_w_o[j])
        else:
            h = _ssd_layer(h, norm_mix_g[i], ssd_w_in[j], ssd_conv_w[j], ssd_conv_b[j], ssd_dt_bias[j],
                           ssd_a_log[j], ssd_d[j], ssd_norm_g[j], ssd_w_o[j], batch, seq)
        h = _mlp(h, norm_mlp_g[i], mlp_w_in[i].astype(BF16), mlp_w_out[i].astype(BF16), tm=512, th=512)
    return h.reshape(batch, seq, d)
```

```python
import functools

import jax
import jax.numpy as jnp
import numpy as np
from jax import lax
from jax.experimental import pallas as pl
from jax.experimental.pallas import tpu as pltpu

F32 = jnp.float32
BF16 = jnp.bfloat16

EPS = 1e-6
LANE = 128
SUBLANE = 8
N_MIXERS = 3

SB_HEAD_DIM = 128
GM_CHUNK = 128
GM_GROUP_DIM = 128
SSD_HEAD_DIM = 64
SSD_GROUPS = 8
SSD_STATE = 128
SSD_CONV = 4
SSD_CHUNK = 128

VMEM_LIMIT_BYTES = 48 * 1024 * 1024


def _cparams(*sem):
    return pltpu.CompilerParams(dimension_semantics=sem, vmem_limit_bytes=VMEM_LIMIT_BYTES)


def _rms(x, g):
    ms = jnp.mean(x * x, axis=-1, keepdims=True)
    return x * lax.rsqrt(ms + EPS) * g


def _softplus(x):
    return jnp.maximum(x, 0.0) + jnp.log1p(jnp.exp(-jnp.abs(x)))


def _silu(x):
    return x * jax.nn.sigmoid(x)


def _split3(x):
    hi = x.astype(BF16)
    r = x - hi.astype(F32)
    mid = r.astype(BF16)
    lo = (r - mid.astype(F32)).astype(BF16)
    return hi, mid, lo


def _dot(a, b):
    return jnp.dot(a, b, preferred_element_type=F32)


def _dot_nt(a, b):
    return lax.dot_general(a, b, (((1,), (1,)), ((), ())), preferred_element_type=F32)


def _dot3_lhs(x, m):
    hi, mid, lo = _split3(x)
    return _dot(hi, m) + _dot(mid, m) + _dot(lo, m)


def _dot3_rhs(m, x):
    hi, mid, lo = _split3(x)
    return _dot(m, hi) + _dot(m, mid) + _dot(m, lo)


def _norm_mm_kernel(x_ref, g_ref, w_ref, o_ref, xn_ref, *, act):
    @pl.when(pl.program_id(1) == 0)
    def _():
        xn_ref[...] = _rms(x_ref[...], g_ref[...]).astype(BF16)

    acc = _dot(xn_ref[...], w_ref[...])
    if act == "gelu":
        acc = 0.5 * acc * (1.0 + lax.erf(acc * np.float32(np.sqrt(0.5))))
    o_ref[...] = acc.astype(o_ref.dtype)


def _norm_matmul(x, g, w, layer, *, out_dtype, act=None, tm, tn, n=None):
    t, d = x.shape
    n = w.shape[2] if n is None else n
    return pl.pallas_call(
        functools.partial(_norm_mm_kernel, act=act),
        grid=(t // tm, n // tn),
        in_specs=[
            pl.BlockSpec((tm, d), lambda i, j: (i, 0)),
            pl.BlockSpec((1, d), lambda i, j: (0, 0)),
            pl.BlockSpec((None, d, tn), lambda i, j: (layer, 0, j)),
        ],
        out_specs=pl.BlockSpec((tm, tn), lambda i, j: (i, j)),
        out_shape=jax.ShapeDtypeStruct((t, n), out_dtype),
        scratch_shapes=[pltpu.VMEM((tm, d), BF16)],
        compiler_params=_cparams("parallel", "arbitrary"),
        name="norm_matmul",
    )(x, g.reshape(1, d), w)


def _qkv_kernel(x_ref, g_ref, w_ref, qg_ref, kg_ref, o_ref, xn_ref, *, tiles_per_part):
    j = pl.program_id(1)

    @pl.when(j == 0)
    def _():
        xn_ref[...] = _rms(x_ref[...], g_ref[...]).astype(BF16)

    acc = _dot(xn_ref[...], w_ref[...])
    heads_per_tile = acc.shape[1] // SB_HEAD_DIM

    def head_norm(gain):
        for c in range(heads_per_tile):
            cols = slice(c * SB_HEAD_DIM, (c + 1) * SB_HEAD_DIM)
            o_ref[:, cols] = _rms(acc[:, cols], gain).astype(o_ref.dtype)

    @pl.when(j < tiles_per_part)
    def _():
        head_norm(qg_ref[...])

    @pl.when(jnp.logical_and(j >= tiles_per_part, j < 2 * tiles_per_part))
    def _():
        head_norm(kg_ref[...])

    @pl.when(j >= 2 * tiles_per_part)
    def _():
        o_ref[...] = acc.astype(o_ref.dtype)


def _qkv_proj(x, g, w, layer, q_g, k_g, *, tm, tn):
    t, d = x.shape
    n = w.shape[2]
    width = n // 3
    return pl.pallas_call(
        functools.partial(_qkv_kernel, tiles_per_part=width // tn),
        grid=(t // tm, n // tn),
        in_specs=[
            pl.BlockSpec((tm, d), lambda i, j: (i, 0)),
            pl.BlockSpec((1, d), lambda i, j: (0, 0)),
            pl.BlockSpec((None, d, tn), lambda i, j: (layer, 0, j)),
            pl.BlockSpec((1, SB_HEAD_DIM), lambda i, j: (0, 0)),
            pl.BlockSpec((1, SB_HEAD_DIM), lambda i, j: (0, 0)),
        ],
        out_specs=pl.BlockSpec((tm, tn), lambda i, j: (i, j)),
        out_shape=jax.ShapeDtypeStruct((t, n), BF16),
        scratch_shapes=[pltpu.VMEM((tm, d), BF16)],
        compiler_params=_cparams("parallel", "arbitrary"),
        name="qkv_proj",
    )(x, g.reshape(1, d), w, q_g.reshape(1, -1), k_g.reshape(1, -1))


def _sb_attn_kernel(q_ref, k_ref, v_ref, u_ref, o_ref, *, tq, hp):
    qi = pl.program_id(2)
    d = SB_HEAD_DIM
    upper = u_ref[...]
    qs = [q_ref[:, h * d:(h + 1) * d] for h in range(hp)]

    def tile(j, carries, accs, diagonal):
        start = pl.multiple_of(j * tq, tq)
        if diagonal:
            row = lax.broadcasted_iota(jnp.int32, (tq, tq), 0)
            col = lax.broadcasted_iota(jnp.int32, (tq, tq), 1)
            causal = col < row
        log_betas, log_1ms, parts = [], [], []
        for h in range(hp):
            z = _dot_nt(qs[h], k_ref[pl.ds(start, tq), h * d:(h + 1) * d])
            lse = jnp.log(1.0 + jnp.exp(-jnp.abs(z)))
            log_beta = jnp.minimum(z, 0.0) - lse
            log_1m = log_beta - z
            if diagonal:
                log_1m = jnp.where(causal, log_1m, 0.0)
            parts.append(log_1m.astype(BF16))
            log_betas.append(log_beta)
            log_1ms.append(log_1m)
        suffix = _dot(jnp.concatenate(parts, axis=0), upper)
        new_carries, new_accs = [], []
        for h in range(hp):
            a = jnp.exp(log_betas[h] + suffix[h * tq:(h + 1) * tq] + carries[h])
            if diagonal:
                a = jnp.where(causal, a, 0.0)
            vb = v_ref[pl.ds(start, tq), h * d:(h + 1) * d]
            new_accs.append(accs[h] + _dot(a.astype(BF16), vb))
            new_carries.append(carries[h] + jnp.sum(log_1ms[h], axis=-1, keepdims=True))
        return tuple(new_carries), tuple(new_accs)

    carries = tuple(jnp.zeros((tq, 1), F32) for _ in range(hp))
    accs = tuple(jnp.zeros((tq, d), F32) for _ in range(hp))
    carries, accs = tile(qi, carries, accs, True)

    carries, accs = lax.cond(
        qi % 2 == 1,
        lambda c: tile(qi - 1, c[0], c[1], False),
        lambda c: c,
        (carries, accs))
    top = qi - qi % 2

    def body(step, c):
        c = tile(top - 1 - 2 * step, c[0], c[1], False)
        return tile(top - 2 - 2 * step, c[0], c[1], False)

    carries, accs = lax.fori_loop(0, qi // 2, body, (carries, accs))
    for h in range(hp):
        o_ref[:, h * d:(h + 1) * d] = accs[h].astype(o_ref.dtype)


def _sb_attention(qkv, batch, seq, *, tq, hp):
    t, n = qkv.shape
    heads = n // (3 * SB_HEAD_DIM)
    hg = heads // hp
    nq = seq // tq
    w = hp * SB_HEAD_DIM
    s_idx = np.arange(tq)[:, None]
    j_idx = np.arange(tq)[None, :]
    upper = jnp.asarray((s_idx > j_idx).astype(np.float32), dtype=BF16)
    return pl.pallas_call(
        functools.partial(_sb_attn_kernel, tq=tq, hp=hp),
        grid=(batch, hg, nq),
        in_specs=[
            pl.BlockSpec((tq, w), lambda b, h, i: (b * nq + i, h)),
            pl.BlockSpec((seq, w), lambda b, h, i: (b, hg + h)),
            pl.BlockSpec((seq, w), lambda b, h, i: (b, 2 * hg + h)),
            pl.BlockSpec((tq, tq), lambda b, h, i: (0, 0)),
        ],
        out_specs=pl.BlockSpec((tq, w), lambda b, h, i: (b * nq + i, h)),
        out_shape=jax.ShapeDtypeStruct((t, heads * SB_HEAD_DIM), BF16),
        compiler_params=pltpu.CompilerParams(
            dimension_semantics=("parallel", "parallel", "arbitrary"),
            vmem_limit_bytes=VMEM_LIMIT_BYTES),
        name="sb_attention",
    )(qkv, qkv, qkv, upper)


def _mm_res_kernel(a_ref, w_ref, r_ref, o_ref):
    o_ref[...] = r_ref[...] + _dot(a_ref[...], w_ref[...])


def _matmul_residual(a, w, layer, res, *, tm, tn):
    t, k = a.shape
    n = w.shape[2]
    return pl.pallas_call(
        _mm_res_kernel,
        grid=(t // tm, n // tn),
        in_specs=[
            pl.BlockSpec((tm, k), lambda i, j: (i, 0)),
            pl.BlockSpec((None, k, tn), lambda i, j: (layer, 0, j)),
            pl.BlockSpec((tm, tn), lambda i, j: (i, j)),
        ],
        out_specs=pl.BlockSpec((tm, tn), lambda i, j: (i, j)),
        out_shape=jax.ShapeDtypeStruct((t, n), F32),
        compiler_params=_cparams("parallel", "arbitrary"),
        name="matmul_residual",
    )(a, w, res)


def _mlp_kernel(x_ref, g_ref, w1_ref, w2_ref, o_ref, xn_ref):
    @pl.when(pl.program_id(1) == 0)
    def _():
        x = x_ref[...]
        xn_ref[...] = _rms(x, g_ref[...]).astype(BF16)
        o_ref[...] = x

    h = _dot(xn_ref[...], w1_ref[...])
    h = jnp.square(jnp.maximum(h, 0.0))
    o_ref[...] += _dot(h.astype(BF16), w2_ref[...])


def _mlp(x, g, w1, w2, layer, *, tm, th):
    t, d = x.shape
    hidden = w1.shape[2]
    return pl.pallas_call(
        _mlp_kernel,
        grid=(t // tm, hidden // th),
        in_specs=[
            pl.BlockSpec((tm, d), lambda i, k: (i, 0)),
            pl.BlockSpec((1, d), lambda i, k: (0, 0)),
            pl.BlockSpec((None, d, th), lambda i, k: (layer, 0, k)),
            pl.BlockSpec((None, th, d), lambda i, k: (layer, k, 0)),
        ],
        out_specs=pl.BlockSpec((tm, d), lambda i, k: (i, 0)),
        out_shape=jax.ShapeDtypeStruct((t, d), F32),
        scratch_shapes=[pltpu.VMEM((tm, d), BF16)],
        compiler_params=_cparams("parallel", "arbitrary"),
        name="relu2_mlp",
    )(x, g.reshape(1, d), w1, w2)


def _gm_gate_kernel(u_ref, v_ref, vg_ref, ws_ref, bst_ref, x_ref, wo_ref, o_ref, vn_ref, y_ref):
    tm = u_ref.shape[0]
    groups = ws_ref.shape[0]
    vn_ref[...] = _rms(v_ref[...].astype(F32), vg_ref[...]).astype(BF16)
    row = lax.broadcasted_iota(jnp.int32, (GM_CHUNK, GM_CHUNK), 0)
    col = lax.broadcasted_iota(jnp.int32, (GM_CHUNK, GM_CHUNK), 1)
    causal = col <= row
    for g in range(groups):
        cols = slice(g * GM_GROUP_DIM, (g + 1) * GM_GROUP_DIM)
        w = jnp.where(causal, ws_ref[g], 0.0).astype(BF16)
        bias = bst_ref[:, g:g + 1]
        for c in range(tm // GM_CHUNK):
            rows = slice(c * GM_CHUNK, (c + 1) * GM_CHUNK)
            mixed = _dot(w, vn_ref[rows, cols]) + bias
            y_ref[rows, cols] = (u_ref[rows, cols].astype(F32) * mixed).astype(BF16)
    o_ref[...] = x_ref[...] + _dot(y_ref[...], wo_ref[...])


def _gm_gate(uv, v_g, w_s, b_s, x, w_o, layer, *, tm):
    t, d = x.shape
    width = uv.shape[1] // 2
    groups = w_s.shape[0]
    return pl.pallas_call(
        _gm_gate_kernel,
        grid=(t // tm,),
        in_specs=[
            pl.BlockSpec((tm, width), lambda i: (i, 0)),
            pl.BlockSpec((tm, width), lambda i: (i, 1)),
            pl.BlockSpec((1, width), lambda i: (0, 0)),
            pl.BlockSpec((groups, GM_CHUNK, GM_CHUNK), lambda i: (0, 0, 0)),
            pl.BlockSpec((GM_CHUNK, groups), lambda i: (0, 0)),
            pl.BlockSpec((tm, d), lambda i: (i, 0)),
            pl.BlockSpec((None, width, d), lambda i: (layer, 0, 0)),
        ],
        out_specs=pl.BlockSpec((tm, d), lambda i: (i, 0)),
        out_shape=jax.ShapeDtypeStruct((t, d), F32),
        scratch_shapes=[pltpu.VMEM((tm, width), BF16), pltpu.VMEM((tm, width), BF16)],
        compiler_params=_cparams("parallel"),
        name="gmlp_gate",
    )(uv, uv, v_g.reshape(1, width), w_s, b_s.T, x, w_o)


def _causal_conv_silu(x, tail, w, b):
    rows = x.shape[0]
    acc = x * w[SSD_CONV - 1:SSD_CONV, :]
    top = x[0:SUBLANE] * w[SSD_CONV - 1:SSD_CONV, :]
    row8 = lax.broadcasted_iota(jnp.int32, (SUBLANE, x.shape[1]), 0)
    for s in range(1, SSD_CONV):
        wk = w[SSD_CONV - 1 - s:SSD_CONV - s, :]
        rolled = pltpu.roll(x, s, 0)
        acc = acc + rolled * wk
        prev = pltpu.roll(tail, s, 0)
        top = top + jnp.where(row8 < s, prev, rolled[0:SUBLANE]) * wk
    y = jnp.concatenate([top, acc[SUBLANE:rows]], axis=0) + b
    return _silu(y)


def _ssd_kernel(z_ref, xi_ref, bm_ref, cm_ref, dt_ref,
                wx_ref, wb_ref, wc_ref, bx_ref, bb_ref, bc_ref,
                dtb_ref, alog_ref, e_ref, dskip_ref, ng_ref, tri_ref,
                o_ref,
                state_ref, tx_ref, tb_ref, tc_ref, dts_ref, ac_ref, act_ref):
    c = pl.program_id(1)
    g = pl.program_id(2)
    L = SSD_CHUNK
    hpg = e_ref.shape[1] // SSD_HEAD_DIM

    @pl.when(c == 0)
    def _():
        state_ref[g] = jnp.zeros(state_ref.shape[1:], F32)
        tx_ref[g] = jnp.zeros(tx_ref.shape[1:], F32)
        tb_ref[g] = jnp.zeros(tb_ref.shape[1:], F32)
        tc_ref[g] = jnp.zeros(tc_ref.shape[1:], F32)

    @pl.when(g == 0)
    def _():
        dt = _softplus(dt_ref[...] + dtb_ref[...])
        a = -jnp.exp(alog_ref[...])
        a_cum = _dot3_rhs(tri_ref[...], dt * a)
        dts_ref[...] = dt
        ac_ref[...] = a_cum
        act_ref[...] = a_cum.T

    x_raw = xi_ref[...]
    b_raw = bm_ref[...]
    c_raw = cm_ref[...]
    xi = _causal_conv_silu(x_raw, tx_ref[g], wx_ref[...], bx_ref[...])
    bm = _causal_conv_silu(b_raw, tb_ref[g], wb_ref[...], bb_ref[...])
    cm = _causal_conv_silu(c_raw, tc_ref[g], wc_ref[...], bc_ref[...])
    tx_ref[g] = x_raw[L - SUBLANE:L]
    tb_ref[g] = b_raw[L - SUBLANE:L]
    tc_ref[g] = c_raw[L - SUBLANE:L]

    expand = e_ref[...]
    a_cum = ac_ref[...]
    dt_x = _dot3_lhs(dts_ref[...], expand)
    ac_x = _dot3_lhs(a_cum, expand)
    last_x = ac_x[L - 1:L, :]
    xs = xi * dt_x
    xsd = xs * jnp.exp(last_x - ac_x)

    cm_b = cm.astype(BF16)
    bm_b = bm.astype(BF16)
    xs_b = xs.astype(BF16)
    state = state_ref[g]
    y = _dot(cm_b, state.astype(BF16)) * jnp.exp(ac_x)
    state_ref[g] = jnp.exp(last_x) * state + _dot(bm.T.astype(BF16), xsd.astype(BF16))

    cb = _dot_nt(cm_b, bm_b)
    row = lax.broadcasted_iota(jnp.int32, (L, L), 0)
    col = lax.broadcasted_iota(jnp.int32, (L, L), 1)
    tri = col <= row
    lane = lax.broadcasted_iota(jnp.int32, (L, 2 * SSD_HEAD_DIM), 1)
    first = lane < SSD_HEAD_DIM
    y_diag = []
    for pair in range(hpg // 2):
        outs = []
        for r in (2 * pair, 2 * pair + 1):
            a_col = ac_x[:, r * SSD_HEAD_DIM:r * SSD_HEAD_DIM + 1]
            a_row = act_ref[pl.ds(g * hpg + r, 1), :]
            decay = jnp.exp(jnp.where(tri, a_col - a_row, -jnp.inf))
            m = (cb * decay).astype(BF16)
            outs.append(_dot(m, xs_b[:, pair * 2 * SSD_HEAD_DIM:(pair + 1) * 2 * SSD_HEAD_DIM]))
        y_diag.append(jnp.where(first, outs[0], outs[1]))
    y = y + jnp.concatenate(y_diag, axis=1)
    y = y + dskip_ref[...] * xi
    y = y * _silu(z_ref[...])
    o_ref[...] = _rms(y, ng_ref[...]).astype(o_ref.dtype)


def _ssd_scan(zxbc, dt_raw, conv_w, conv_b, dt_bias, a_log, d_skip, norm_g, batch, seq):
    t = zxbc.shape[0]
    heads = d_skip.shape[0]
    inner = heads * SSD_HEAD_DIM
    gw = inner // SSD_GROUPS
    hpg = heads // SSD_GROUPS
    nc = seq // SSD_CHUNK
    hp = dt_raw.shape[1]
    L = SSD_CHUNK

    pad = hp - heads
    dtb = jnp.pad(dt_bias, (0, pad)).reshape(1, hp)
    alog = jnp.pad(a_log, (0, pad)).reshape(1, hp)
    e = np.zeros((hp, inner), np.float32)
    for h in range(heads):
        e[h, h * SSD_HEAD_DIM:(h + 1) * SSD_HEAD_DIM] = 1.0
    e = jnp.asarray(e, dtype=BF16)
    tri = jnp.asarray(np.tril(np.ones((L, L), np.float32)), dtype=BF16)
    dskip_x = jnp.repeat(d_skip, SSD_HEAD_DIM).reshape(1, inner)
    conv_b2 = conv_b.reshape(1, -1)

    zb = inner // gw
    xb = 0
    nb = inner // SSD_STATE
    row_blk = lambda b, c, g: b * nc + c
    in_specs = [
        pl.BlockSpec((L, gw), lambda b, c, g: (row_blk(b, c, g), g)),
        pl.BlockSpec((L, gw), lambda b, c, g: (row_blk(b, c, g), zb + g)),
        pl.BlockSpec((L, SSD_STATE), lambda b, c, g: (row_blk(b, c, g), 2 * nb + g)),
        pl.BlockSpec((L, SSD_STATE), lambda b, c, g: (row_blk(b, c, g), 2 * nb + SSD_GROUPS + g)),
        pl.BlockSpec((L, hp), lambda b, c, g: (row_blk(b, c, g), 0)),
        pl.BlockSpec((SSD_CONV, gw), lambda b, c, g: (0, xb + g)),
        pl.BlockSpec((SSD_CONV, SSD_STATE), lambda b, c, g: (0, nb + g)),
        pl.BlockSpec((SSD_CONV, SSD_STATE), lambda b, c, g: (0, nb + SSD_GROUPS + g)),
        pl.BlockSpec((1, gw), lambda b, c, g: (0, xb + g)),
        pl.BlockSpec((1, SSD_STATE), lambda b, c, g: (0, nb + g)),
        pl.BlockSpec((1, SSD_STATE), lambda b, c, g: (0, nb + SSD_GROUPS + g)),
        pl.BlockSpec((1, hp), lambda b, c, g: (0, 0)),
        pl.BlockSpec((1, hp), lambda b, c, g: (0, 0)),
        pl.BlockSpec((hp, gw), lambda b, c, g: (0, g)),
        pl.BlockSpec((1, gw), lambda b, c, g: (0, g)),
        pl.BlockSpec((1, gw), lambda b, c, g: (0, g)),
        pl.BlockSpec((L, L), lambda b, c, g: (0, 0)),
    ]
    return pl.pallas_call(
        _ssd_kernel,
        grid=(batch, nc, SSD_GROUPS),
        in_specs=in_specs,
        out_specs=pl.BlockSpec((L, gw), lambda b, c, g: (row_blk(b, c, g), g)),
        out_shape=jax.ShapeDtypeStruct((t, inner), BF16),
        scratch_shapes=[
            pltpu.VMEM((SSD_GROUPS, SSD_STATE, gw), F32),
            pltpu.VMEM((SSD_GROUPS, SUBLANE, gw), F32),
            pltpu.VMEM((SSD_GROUPS, SUBLANE, SSD_STATE), F32),
            pltpu.VMEM((SSD_GROUPS, SUBLANE, SSD_STATE), F32),
            pltpu.VMEM((L, hp), F32),
            pltpu.VMEM((L, hp), F32),
            pltpu.VMEM((hp, L), F32),
        ],
        compiler_params=_cparams("arbitrary", "arbitrary", "arbitrary"),
        name="ssd_scan",
    )(zxbc, zxbc, zxbc, zxbc, dt_raw,
      conv_w, conv_w, conv_w, conv_b2, conv_b2, conv_b2,
      dtb, alog, e, dskip_x, norm_g.reshape(1, inner), tri)


def _sb_layer(h, norm_g, w_qkv, q_g, k_g, w_o, j, batch, seq):
    q_gain = q_g * np.float32(SB_HEAD_DIM ** -0.5)
    qkv = _qkv_proj(h, norm_g, w_qkv, j, q_gain, k_g, tm=1024, tn=512)
    o = _sb_attention(qkv, batch, seq, tq=256, hp=4)
    return _matmul_residual(o, w_o, j, h, tm=1024, tn=512)


def _gm_layer(h, norm_g, w_in, v_g, w_s, b_s, w_o, j):
    uv = _norm_matmul(h, norm_g, w_in, j, out_dtype=BF16, act="gelu", tm=1024, tn=512)
    return _gm_gate(uv, v_g, w_s, b_s, h, w_o, j, tm=256)


def _ssd_layer(h, norm_g, w_in, w_in_f32, conv_w, conv_b, dt_bias, a_log, d_skip, ssd_norm_g, w_o, j,
               batch, seq):
    heads = d_skip.shape[0]
    main = w_in.shape[2] - heads
    hp = -(-heads // LANE) * LANE
    w_dt = jnp.pad(w_in_f32[j, :, main:], ((0, 0), (0, hp - heads))).astype(BF16)[None]
    zxbc = _norm_matmul(h, norm_g, w_in, j, out_dtype=F32, tm=1024, tn=512, n=main)
    dt_raw = _norm_matmul(h, norm_g, w_dt, 0, out_dtype=F32, tm=1024, tn=hp)
    y = _ssd_scan(zxbc, dt_raw, conv_w, conv_b, dt_bias, a_log, d_skip, ssd_norm_g, batch, seq)
    return _matmul_residual(y, w_o, j, h, tm=1024, tn=512)


def kernel(x, norm_mix_g, norm_mlp_g, sb_w_qkv, sb_q_norm_g, sb_k_norm_g, sb_w_o, gm_w_in, gm_v_norm_g, gm_w_s, gm_b_s, gm_w_o, ssd_w_in, ssd_conv_w, ssd_conv_b, ssd_dt_bias, ssd_a_log, ssd_d, ssd_norm_g, ssd_w_o, mlp_w_in, mlp_w_out):
    batch, seq, d = x.shape
    depth = norm_mix_g.shape[0]
    sb_w_qkv_b, sb_w_o_b = sb_w_qkv.astype(BF16), sb_w_o.astype(BF16)
    gm_w_in_b, gm_w_o_b = gm_w_in.astype(BF16), gm_w_o.astype(BF16)
    ssd_w_in_b, ssd_w_o_b = ssd_w_in.astype(BF16), ssd_w_o.astype(BF16)
    mlp_w_in_b, mlp_w_out_b = mlp_w_in.astype(BF16), mlp_w_out.astype(BF16)
    h = x.reshape(batch * seq, d)
    for i in range(depth):
        kind, j = i % N_MIXERS, i // N_MIXERS
        if kind == 0:
            h = _sb_layer(h, norm_mix_g[i], sb_w_qkv_b, sb_q_norm_g[j], sb_k_norm_g[j], sb_w_o_b, j, batch, seq)
        elif kind == 1:
            h = _gm_layer(h, norm_mix_g[i], gm_w_in_b, gm_v_norm_g[j], gm_w_s[j], gm_b_s[j], gm_w_o_b, j)
        else:
            h = _ssd_layer(h, norm_mix_g[i], ssd_w_in_b, ssd_w_in, ssd_conv_w[j], ssd_conv_b[j],
                           ssd_dt_bias[j], ssd_a_log[j], ssd_d[j], ssd_norm_g[j], ssd_w_o_b, j, batch, seq)
        h = _mlp(h, norm_mlp_g[i], mlp_w_in_b, mlp_w_out_b, i, tm=512, th=1024)
    return h.reshape(batch, seq, d)
```

```python
import functools

import jax
import jax.numpy as jnp
import numpy as np
from jax import lax
from jax.experimental import pallas as pl
from jax.experimental.pallas import tpu as pltpu

F32 = jnp.float32
BF16 = jnp.bfloat16

EPS = 1e-6
LOG2E = float(np.log2(np.e))
LANE = 128
SUBLANE = 8
N_MIXERS = 3

SB_HEAD_DIM = 128
GM_CHUNK = 128
GM_GROUP_DIM = 128
SSD_HEAD_DIM = 64
SSD_GROUPS = 8
SSD_STATE = 128
SSD_CONV = 4
SSD_CHUNK = 128

VMEM_LIMIT_BYTES = 48 * 1024 * 1024
ROW_CHUNKS = 4


def _cparams(*sem):
    return pltpu.CompilerParams(dimension_semantics=sem, vmem_limit_bytes=VMEM_LIMIT_BYTES)


def _rms(x, g):
    ms = jnp.mean(x * x, axis=-1, keepdims=True)
    return x * lax.rsqrt(ms + EPS) * g


def _softplus(x):
    return jnp.maximum(x, 0.0) + jnp.log1p(jnp.exp(-jnp.abs(x)))


def _silu(x):
    return x * jax.nn.sigmoid(x)


def _split3(x):
    hi = x.astype(BF16)
    r = x - hi.astype(F32)
    mid = r.astype(BF16)
    lo = (r - mid.astype(F32)).astype(BF16)
    return hi, mid, lo


def _dot(a, b):
    return jnp.dot(a, b, preferred_element_type=F32)


def _dot_nt(a, b):
    return lax.dot_general(a, b, (((1,), (1,)), ((), ())), preferred_element_type=F32)


def _dot3_rhs(m, x):
    hi, mid, lo = _split3(x)
    return _dot(m, hi) + _dot(m, mid) + _dot(m, lo)


def _norm_mm_kernel(x_ref, g_ref, w_ref, o_ref, xn_ref, *, act):
    @pl.when(pl.program_id(1) == 0)
    def _():
        xn_ref[...] = _rms(x_ref[...], g_ref[...]).astype(BF16)

    w = w_ref[...]
    chunk = xn_ref.shape[0] // ROW_CHUNKS
    for r in range(ROW_CHUNKS):
        rows = slice(r * chunk, (r + 1) * chunk)
        acc = _dot(xn_ref[rows, :], w)
        if act == "gelu":
            acc = 0.5 * acc * (1.0 + lax.erf(acc * np.float32(np.sqrt(0.5))))
        o_ref[rows, :] = acc.astype(o_ref.dtype)


def _norm_matmul(x, g, w, layer, *, out_dtype, act=None, tm, tn, n=None):
    t, d = x.shape
    n = w.shape[2] if n is None else n
    return pl.pallas_call(
        functools.partial(_norm_mm_kernel, act=act),
        grid=(t // tm, n // tn),
        in_specs=[
            pl.BlockSpec((tm, d), lambda i, j: (i, 0)),
            pl.BlockSpec((1, d), lambda i, j: (0, 0)),
            pl.BlockSpec((None, d, tn), lambda i, j: (layer, 0, j)),
        ],
        out_specs=pl.BlockSpec((tm, tn), lambda i, j: (i, j)),
        out_shape=jax.ShapeDtypeStruct((t, n), out_dtype),
        scratch_shapes=[pltpu.VMEM((tm, d), BF16)],
        compiler_params=_cparams("parallel", "arbitrary"),
        name="norm_matmul",
    )(x, g.reshape(1, d), w)


def _qkv_kernel(x_ref, g_ref, w_ref, hg_ref, o_ref, xn_ref, *, tiles_per_part):
    j = pl.program_id(1)

    @pl.when(j == 0)
    def _():
        xn_ref[...] = _rms(x_ref[...], g_ref[...]).astype(BF16)

    is_v = j >= 2 * tiles_per_part
    gain = hg_ref[...]
    w = w_ref[...]
    half = xn_ref.shape[0] // ROW_CHUNKS
    for r in range(ROW_CHUNKS):
        rows = slice(r * half, (r + 1) * half)
        acc = _dot(xn_ref[rows, :], w)
        for c in range(acc.shape[1] // SB_HEAD_DIM):
            cols = slice(c * SB_HEAD_DIM, (c + 1) * SB_HEAD_DIM)
            blk = acc[:, cols]
            ms = jnp.mean(blk * blk, axis=-1, keepdims=True)
            factor = jnp.where(is_v, 1.0, lax.rsqrt(ms + EPS))
            o_ref[rows, cols] = (blk * factor * gain).astype(o_ref.dtype)


def _qkv_proj(x, g, w, layer, q_g, k_g, *, tm, tn):
    t, d = x.shape
    n = w.shape[2]
    tiles_per_part = n // 3 // tn
    head_gains = jnp.stack([q_g, k_g, jnp.ones_like(q_g)]).reshape(3, 1, SB_HEAD_DIM)
    return pl.pallas_call(
        functools.partial(_qkv_kernel, tiles_per_part=tiles_per_part),
        grid=(t // tm, n // tn),
        in_specs=[
            pl.BlockSpec((tm, d), lambda i, j: (i, 0)),
            pl.BlockSpec((1, d), lambda i, j: (0, 0)),
            pl.BlockSpec((None, d, tn), lambda i, j: (layer, 0, j)),
            pl.BlockSpec((None, 1, SB_HEAD_DIM), lambda i, j: (j // tiles_per_part, 0, 0)),
        ],
        out_specs=pl.BlockSpec((tm, tn), lambda i, j: (i, j)),
        out_shape=jax.ShapeDtypeStruct((t, n), BF16),
        scratch_shapes=[pltpu.VMEM((tm, d), BF16)],
        compiler_params=_cparams("parallel", "arbitrary"),
        name="qkv_proj",
    )(x, g.reshape(1, d), w, head_gains)


def _sb_attn_kernel(q_ref, k_ref, v_ref, u_ref, o_ref, lb_ref, l1m_ref, *, tq, hp):
    qi = pl.program_id(2)
    d = SB_HEAD_DIM
    upper = u_ref[...]
    qs = [q_ref[:, h * d:(h + 1) * d] for h in range(hp)]

    def logits_stage(j, slot, diagonal):
        start = pl.multiple_of(j * tq, tq)
        if diagonal:
            row = lax.broadcasted_iota(jnp.int32, (tq, tq), 0)
            col = lax.broadcasted_iota(jnp.int32, (tq, tq), 1)
            causal = col < row
        for h in range(hp):
            z = _dot_nt(qs[h], k_ref[pl.ds(start, tq), h * d:(h + 1) * d])
            lse = jnp.log(1.0 + jnp.exp2(-jnp.abs(z))) * LOG2E
            log_beta = jnp.minimum(z, 0.0) - lse
            log_1m = log_beta - z
            if diagonal:
                log_1m = jnp.where(causal, log_1m, 0.0)
                log_beta = jnp.where(causal, log_beta, -jnp.inf)
            lb_ref[slot, h] = log_beta
            l1m_ref[slot, h * tq:(h + 1) * tq, :] = log_1m.astype(BF16)

    def weights_stage(j, slot, carries, accs):
        start = pl.multiple_of(j * tq, tq)
        l1m = l1m_ref[slot]
        suffix = _dot(l1m, upper)
        new_carries, new_accs = [], []
        for h in range(hp):
            rows = slice(h * tq, (h + 1) * tq)
            a = jnp.exp2(lb_ref[slot, h] + suffix[rows] + carries[h])
            vb = v_ref[pl.ds(start, tq), h * d:(h + 1) * d]
            new_accs.append(accs[h] + _dot(a.astype(BF16), vb))
            new_carries.append(carries[h] + suffix[rows, 0:1] + l1m[rows, 0:1].astype(F32))
        return tuple(new_carries), tuple(new_accs)

    carries = tuple(jnp.zeros((tq, 1), F32) for _ in range(hp))
    accs = tuple(jnp.zeros((tq, d), F32) for _ in range(hp))
    odd = qi % 2
    logits_stage(qi, odd, True)

    def extra_trip(c):
        logits_stage(qi - 1, 0, False)
        return weights_stage(qi, 1, c[0], c[1])

    carries, accs = lax.cond(odd == 1, extra_trip, lambda c: c, (carries, accs))
    top = qi - odd

    def body(step, c):
        j = top - 2 * step
        logits_stage(j - 1, 1, False)
        c = weights_stage(j, 0, c[0], c[1])
        logits_stage(j - 2, 0, False)
        return weights_stage(j - 1, 1, c[0], c[1])

    carries, accs = lax.fori_loop(0, top // 2, body, (carries, accs))
    carries, accs = weights_stage(0, 0, carries, accs)
    for h in range(hp):
        o_ref[:, h * d:(h + 1) * d] = accs[h].astype(o_ref.dtype)


def _sb_attention(qkv, batch, seq, *, tq, hp):
    t, n = qkv.shape
    heads = n // (3 * SB_HEAD_DIM)
    hg = heads // hp
    nq = seq // tq
    w = hp * SB_HEAD_DIM
    s_idx = np.arange(tq)[:, None]
    j_idx = np.arange(tq)[None, :]
    upper = jnp.asarray((s_idx > j_idx).astype(np.float32), dtype=BF16)
    return pl.pallas_call(
        functools.partial(_sb_attn_kernel, tq=tq, hp=hp),
        grid=(batch, hg, nq),
        in_specs=[
            pl.BlockSpec((tq, w), lambda b, h, i: (b * nq + i, h)),
            pl.BlockSpec((seq, w), lambda b, h, i: (b, hg + h)),
            pl.BlockSpec((seq, w), lambda b, h, i: (b, 2 * hg + h)),
            pl.BlockSpec((tq, tq), lambda b, h, i: (0, 0)),
        ],
        out_specs=pl.BlockSpec((tq, w), lambda b, h, i: (b * nq + i, h)),
        out_shape=jax.ShapeDtypeStruct((t, heads * SB_HEAD_DIM), BF16),
        scratch_shapes=[pltpu.VMEM((2, hp, tq, tq), F32), pltpu.VMEM((2, hp * tq, tq), BF16)],
        compiler_params=pltpu.CompilerParams(
            dimension_semantics=("parallel", "parallel", "arbitrary"),
            vmem_limit_bytes=VMEM_LIMIT_BYTES),
        name="sb_attention",
    )(qkv, qkv, qkv, upper)


def _mm_res_kernel(a_ref, w_ref, r_ref, o_ref):
    o_ref[...] = r_ref[...] + _dot(a_ref[...], w_ref[...])


def _matmul_residual(a, w, layer, res, *, tm, tn):
    t, k = a.shape
    n = w.shape[2]
    return pl.pallas_call(
        _mm_res_kernel,
        grid=(t // tm, n // tn),
        in_specs=[
            pl.BlockSpec((tm, k), lambda i, j: (i, 0)),
            pl.BlockSpec((None, k, tn), lambda i, j: (layer, 0, j)),
            pl.BlockSpec((tm, tn), lambda i, j: (i, j)),
        ],
        out_specs=pl.BlockSpec((tm, tn), lambda i, j: (i, j)),
        out_shape=jax.ShapeDtypeStruct((t, n), F32),
        compiler_params=_cparams("parallel", "arbitrary"),
        name="matmul_residual",
    )(a, w, res)


def _mlp_kernel(x_ref, g_ref, w1_ref, w2_ref, o_ref, xn_ref):
    @pl.when(pl.program_id(1) == 0)
    def _():
        x = x_ref[...]
        xn_ref[...] = _rms(x, g_ref[...]).astype(BF16)
        o_ref[...] = x

    h = _dot(xn_ref[...], w1_ref[...])
    h = jnp.square(jnp.maximum(h, 0.0))
    o_ref[...] += _dot(h.astype(BF16), w2_ref[...])


def _mlp(x, g, w1, w2, layer, *, tm, th):
    t, d = x.shape
    hidden = w1.shape[2]
    return pl.pallas_call(
        _mlp_kernel,
        grid=(t // tm, hidden // th),
        in_specs=[
            pl.BlockSpec((tm, d), lambda i, k: (i, 0)),
            pl.BlockSpec((1, d), lambda i, k: (0, 0)),
            pl.BlockSpec((None, d, th), lambda i, k: (layer, 0, k)),
            pl.BlockSpec((None, th, d), lambda i, k: (layer, k, 0)),
        ],
        out_specs=pl.BlockSpec((tm, d), lambda i, k: (i, 0)),
        out_shape=jax.ShapeDtypeStruct((t, d), F32),
        scratch_shapes=[pltpu.VMEM((tm, d), BF16)],
        compiler_params=_cparams("parallel", "arbitrary"),
        name="relu2_mlp",
    )(x, g.reshape(1, d), w1, w2)


def _gm_gate_kernel(u_ref, v_ref, vg_ref, ws_ref, bst_ref, x_ref, wo_ref, o_ref, vn_ref, y_ref):
    tm = u_ref.shape[0]
    groups = ws_ref.shape[0]
    vn_ref[...] = _rms(v_ref[...].astype(F32), vg_ref[...]).astype(BF16)
    row = lax.broadcasted_iota(jnp.int32, (GM_CHUNK, GM_CHUNK), 0)
    col = lax.broadcasted_iota(jnp.int32, (GM_CHUNK, GM_CHUNK), 1)
    causal = col <= row
    for g in range(groups):
        cols = slice(g * GM_GROUP_DIM, (g + 1) * GM_GROUP_DIM)
        w = jnp.where(causal, ws_ref[g], 0.0).astype(BF16)
        bias = bst_ref[:, g:g + 1]
        for c in range(tm // GM_CHUNK):
            rows = slice(c * GM_CHUNK, (c + 1) * GM_CHUNK)
            mixed = _dot(w, vn_ref[rows, cols]) + bias
            y_ref[rows, cols] = (u_ref[rows, cols].astype(F32) * mixed).astype(BF16)
    o_ref[...] = x_ref[...] + _dot(y_ref[...], wo_ref[...])


def _gm_gate(uv, v_g, w_s, b_s, x, w_o, layer, *, tm):
    t, d = x.shape
    width = uv.shape[1] // 2
    groups = w_s.shape[0]
    return pl.pallas_call(
        _gm_gate_kernel,
        grid=(t // tm,),
        in_specs=[
            pl.BlockSpec((tm, width), lambda i: (i, 0)),
            pl.BlockSpec((tm, width), lambda i: (i, 1)),
            pl.BlockSpec((1, width), lambda i: (0, 0)),
            pl.BlockSpec((groups, GM_CHUNK, GM_CHUNK), lambda i: (0, 0, 0)),
            pl.BlockSpec((GM_CHUNK, groups), lambda i: (0, 0)),
            pl.BlockSpec((tm, d), lambda i: (i, 0)),
            pl.BlockSpec((None, width, d), lambda i: (layer, 0, 0)),
        ],
        out_specs=pl.BlockSpec((tm, d), lambda i: (i, 0)),
        out_shape=jax.ShapeDtypeStruct((t, d), F32),
        scratch_shapes=[pltpu.VMEM((tm, width), BF16), pltpu.VMEM((tm, width), BF16)],
        compiler_params=_cparams("parallel"),
        name="gmlp_gate",
    )(uv, uv, v_g.reshape(1, width), w_s, b_s.T, x, w_o)


def _causal_conv_silu(x, tail, w, b):
    rows = x.shape[0]
    acc = x * w[SSD_CONV - 1:SSD_CONV, :]
    top = x[0:SUBLANE] * w[SSD_CONV - 1:SSD_CONV, :]
    row8 = lax.broadcasted_iota(jnp.int32, (SUBLANE, x.shape[1]), 0)
    for s in range(1, SSD_CONV):
        wk = w[SSD_CONV - 1 - s:SSD_CONV - s, :]
        rolled = pltpu.roll(x, s, 0)
        acc = acc + rolled * wk
        prev = pltpu.roll(tail, s, 0)
        top = top + jnp.where(row8 < s, prev, rolled[0:SUBLANE]) * wk
    y = jnp.concatenate([top, acc[SUBLANE:rows]], axis=0) + b
    return _silu(y)


def _ssd_kernel(z_ref, xi_ref, bm_ref, cm_ref, dt_ref,
                wx_ref, wb_ref, wc_ref, bx_ref, bb_ref, bc_ref,
                dtb_ref, alog_ref, e_ref, dskip_ref, ng_ref, tri_ref,
                o_ref,
                state_ref, tx_ref, tb_ref, tc_ref):
    L = SSD_CHUNK
    gw = state_ref.shape[2]
    hpg = gw // SSD_HEAD_DIM
    n = SSD_STATE

    @pl.when(pl.program_id(1) == 0)
    def _():
        state_ref[...] = jnp.zeros(state_ref.shape, F32)
        tx_ref[...] = jnp.zeros(tx_ref.shape, F32)
        tb_ref[...] = jnp.zeros(tb_ref.shape, F32)
        tc_ref[...] = jnp.zeros(tc_ref.shape, F32)

    dt = _softplus(dt_ref[...] + dtb_ref[...])
    a = -jnp.exp(alog_ref[...])
    a_cum = _dot3_rhs(tri_ref[...], dt * a)
    a_cum_t = a_cum.T
    dt_parts = _split3(dt)
    ac_parts = _split3(a_cum)

    def expand_heads(parts, sel):
        return _dot(parts[0], sel) + _dot(parts[1], sel) + _dot(parts[2], sel)

    row = lax.broadcasted_iota(jnp.int32, (L, L), 0)
    col = lax.broadcasted_iota(jnp.int32, (L, L), 1)
    tri = col <= row
    lane = lax.broadcasted_iota(jnp.int32, (L, 2 * SSD_HEAD_DIM), 1)
    first = lane < SSD_HEAD_DIM

    for g in range(state_ref.shape[0]):
        gx = slice(g * gw, (g + 1) * gw)
        gn = slice(g * n, (g + 1) * n)
        x_raw = xi_ref[:, gx].astype(F32)
        b_raw = bm_ref[:, gn].astype(F32)
        c_raw = cm_ref[:, gn].astype(F32)
        xi = _causal_conv_silu(x_raw, tx_ref[:, gx], wx_ref[:, gx], bx_ref[:, gx])
        bm = _causal_conv_silu(b_raw, tb_ref[:, gn], wb_ref[:, gn], bb_ref[:, gn])
        cm = _causal_conv_silu(c_raw, tc_ref[:, gn], wc_ref[:, gn], bc_ref[:, gn])
        tx_ref[:, gx] = x_raw[L - SUBLANE:L]
        tb_ref[:, gn] = b_raw[L - SUBLANE:L]
        tc_ref[:, gn] = c_raw[L - SUBLANE:L]

        expand = e_ref[:, gx]
        dt_x = expand_heads(dt_parts, expand)
        ac_x = expand_heads(ac_parts, expand)
        last_x = ac_x[L - 1:L, :]
        xs = xi * dt_x
        xsd = xs * jnp.exp(last_x - ac_x)

        cm_b = cm.astype(BF16)
        bm_b = bm.astype(BF16)
        xs_b = xs.astype(BF16)
        state = state_ref[g]
        y = _dot(cm_b, state.astype(BF16)) * jnp.exp(ac_x)
        state_ref[g] = jnp.exp(last_x) * state + _dot(bm.T.astype(BF16), xsd.astype(BF16))

        cb = _dot_nt(cm_b, bm_b)
        y_diag = []
        for pair in range(hpg // 2):
            outs = []
            for r in (2 * pair, 2 * pair + 1):
                head = g * hpg + r
                a_col = ac_x[:, r * SSD_HEAD_DIM:r * SSD_HEAD_DIM + 1]
                a_row = a_cum_t[head:head + 1, :]
                decay = jnp.exp(jnp.where(tri, a_col - a_row, -jnp.inf))
                m = (cb * decay).astype(BF16)
                outs.append(_dot(m, xs_b[:, pair * 2 * SSD_HEAD_DIM:(pair + 1) * 2 * SSD_HEAD_DIM]))
            y_diag.append(jnp.where(first, outs[0], outs[1]))
        y = y + jnp.concatenate(y_diag, axis=1)
        y = y + dskip_ref[:, gx] * xi
        y = y * _silu(z_ref[:, gx].astype(F32))
        o_ref[:, gx] = _rms(y, ng_ref[:, gx]).astype(o_ref.dtype)


def _ssd_scan(zxbc, dt_raw, conv_w, conv_b, dt_bias, a_log, d_skip, norm_g, batch, seq):
    t = zxbc.shape[0]
    heads = d_skip.shape[0]
    inner = heads * SSD_HEAD_DIM
    gw = inner // SSD_GROUPS
    gn = SSD_GROUPS * SSD_STATE
    nc = seq // SSD_CHUNK
    hp = dt_raw.shape[1]
    L = SSD_CHUNK

    pad = hp - heads
    dtb = jnp.pad(dt_bias, (0, pad)).reshape(1, hp)
    alog = jnp.pad(a_log, (0, pad)).reshape(1, hp)
    e = np.zeros((hp, inner), np.float32)
    for h in range(heads):
        e[h, h * SSD_HEAD_DIM:(h + 1) * SSD_HEAD_DIM] = 1.0
    e = jnp.asarray(e, dtype=BF16)
    tri = jnp.asarray(np.tril(np.ones((L, L), np.float32)), dtype=BF16)
    dskip_x = jnp.repeat(d_skip, SSD_HEAD_DIM).reshape(1, inner)
    conv_b2 = conv_b.reshape(1, -1)

    row_blk = lambda b, c: b * nc + c
    const = lambda b, c: (0, 0)
    in_specs = [
        pl.BlockSpec((L, inner), lambda b, c: (row_blk(b, c), 0)),
        pl.BlockSpec((L, inner), lambda b, c: (row_blk(b, c), 1)),
        pl.BlockSpec((L, gn), lambda b, c: (row_blk(b, c), 2 * inner // gn)),
        pl.BlockSpec((L, gn), lambda b, c: (row_blk(b, c), 2 * inner // gn + 1)),
        pl.BlockSpec((L, hp), lambda b, c: (row_blk(b, c), 0)),
        pl.BlockSpec((SSD_CONV, inner), const),
        pl.BlockSpec((SSD_CONV, gn), lambda b, c: (0, inner // gn)),
        pl.BlockSpec((SSD_CONV, gn), lambda b, c: (0, inner // gn + 1)),
        pl.BlockSpec((1, inner), const),
        pl.BlockSpec((1, gn), lambda b, c: (0, inner // gn)),
        pl.BlockSpec((1, gn), lambda b, c: (0, inner // gn + 1)),
        pl.BlockSpec((1, hp), const),
        pl.BlockSpec((1, hp), const),
        pl.BlockSpec((hp, inner), const),
        pl.BlockSpec((1, inner), const),
        pl.BlockSpec((1, inner), const),
        pl.BlockSpec((L, L), const),
    ]
    return pl.pallas_call(
        _ssd_kernel,
        grid=(batch, nc),
        in_specs=in_specs,
        out_specs=pl.BlockSpec((L, inner), lambda b, c: (row_blk(b, c), 0)),
        out_shape=jax.ShapeDtypeStruct((t, inner), BF16),
        scratch_shapes=[
            pltpu.VMEM((SSD_GROUPS, SSD_STATE, gw), F32),
            pltpu.VMEM((SUBLANE, inner), F32),
            pltpu.VMEM((SUBLANE, gn), F32),
            pltpu.VMEM((SUBLANE, gn), F32),
        ],
        compiler_params=_cparams("arbitrary", "arbitrary"),
        name="ssd_scan",
    )(zxbc, zxbc, zxbc, zxbc, dt_raw,
      conv_w, conv_w, conv_w, conv_b2, conv_b2, conv_b2,
      dtb, alog, e, dskip_x, norm_g.reshape(1, inner), tri)


def _sb_layer(h, norm_g, w_qkv, q_g, k_g, w_o, j, batch, seq):
    q_gain = q_g * np.float32(SB_HEAD_DIM ** -0.5 * LOG2E)
    qkv = _qkv_proj(h, norm_g, w_qkv, j, q_gain, k_g, tm=1024, tn=512)
    o = _sb_attention(qkv, batch, seq, tq=256, hp=4)
    return _matmul_residual(o, w_o, j, h, tm=512, tn=w_o.shape[2])


def _gm_layer(h, norm_g, w_in, v_g, w_s, b_s, w_o, j):
    uv = _norm_matmul(h, norm_g, w_in, j, out_dtype=BF16, act="gelu", tm=1024, tn=512)
    return _gm_gate(uv, v_g, w_s, b_s, h, w_o, j, tm=256)


def _ssd_layer(h, norm_g, w_in, w_in_f32, conv_w, conv_b, dt_bias, a_log, d_skip, ssd_norm_g, w_o, j,
               batch, seq):
    heads = d_skip.shape[0]
    main = w_in.shape[2] - heads
    hp = -(-heads // LANE) * LANE
    w_dt = jnp.pad(w_in_f32[j, :, main:], ((0, 0), (0, hp - heads))).astype(BF16)[None]
    zxbc = _norm_matmul(h, norm_g, w_in, j, out_dtype=BF16, tm=1024, tn=512, n=main)
    dt_raw = _norm_matmul(h, norm_g, w_dt, 0, out_dtype=F32, tm=1024, tn=hp)
    y = _ssd_scan(zxbc, dt_raw, conv_w, conv_b, dt_bias, a_log, d_skip, ssd_norm_g, batch, seq)
    return _matmul_residual(y, w_o, j, h, tm=1024, tn=512)


def kernel(x, norm_mix_g, norm_mlp_g, sb_w_qkv, sb_q_norm_g, sb_k_norm_g, sb_w_o, gm_w_in, gm_v_norm_g, gm_w_s, gm_b_s, gm_w_o, ssd_w_in, ssd_conv_w, ssd_conv_b, ssd_dt_bias, ssd_a_log, ssd_d, ssd_norm_g, ssd_w_o, mlp_w_in, mlp_w_out):
    batch, seq, d = x.shape
    depth = norm_mix_g.shape[0]
    sb_w_qkv_b, sb_w_o_b = sb_w_qkv.astype(BF16), sb_w_o.astype(BF16)
    gm_w_in_b, gm_w_o_b = gm_w_in.astype(BF16), gm_w_o.astype(BF16)
    ssd_w_in_b, ssd_w_o_b = ssd_w_in.astype(BF16), ssd_w_o.astype(BF16)
    mlp_w_in_b, mlp_w_out_b = mlp_w_in.astype(BF16), mlp_w_out.astype(BF16)
    h = x.reshape(batch * seq, d)
    for i in range(depth):
        kind, j = i % N_MIXERS, i // N_MIXERS
        if kind == 0:
            h = _sb_layer(h, norm_mix_g[i], sb_w_qkv_b, sb_q_norm_g[j], sb_k_norm_g[j], sb_w_o_b, j, batch, seq)
        elif kind == 1:
            h = _gm_layer(h, norm_mix_g[i], gm_w_in_b, gm_v_norm_g[j], gm_w_s[j], gm_b_s[j], gm_w_o_b, j)
        else:
            h = _ssd_layer(h, norm_mix_g[i], ssd_w_in_b, ssd_w_in, ssd_conv_w[j], ssd_conv_b[j],
                           ssd_dt_bias[j], ssd_a_log[j], ssd_d[j], ssd_norm_g[j], ssd_w_o_b, j, batch, seq)
        h = _mlp(h, norm_mlp_g[i], mlp_w_in_b, mlp_w_out_b, i, tm=512, th=1024)
    return h.reshape(batch, seq, d)
```

```python
import functools

import jax
import jax.numpy as jnp
import numpy as np
from jax import lax
from jax.experimental import pallas as pl
from jax.experimental.pallas import tpu as pltpu

F32 = jnp.float32
BF16 = jnp.bfloat16

EPS = 1e-6
LOG2E = float(np.log2(np.e))
LANE = 128
SUBLANE = 8
N_MIXERS = 3

SB_HEAD_DIM = 128
GM_CHUNK = 128
GM_GROUP_DIM = 128
SSD_HEAD_DIM = 64
SSD_GROUPS = 8
SSD_STATE = 128
SSD_CONV = 4
SSD_CHUNK = 128

VMEM_LIMIT_BYTES = 48 * 1024 * 1024
ROW_CHUNKS = 4
UNROLL = 4
CAST_SPLIT = 16


def _cparams(*sem):
    return pltpu.CompilerParams(dimension_semantics=sem, vmem_limit_bytes=VMEM_LIMIT_BYTES)


def _rms(x, g):
    ms = jnp.mean(x * x, axis=-1, keepdims=True)
    return x * lax.rsqrt(ms + EPS) * g


def _softplus(x):
    return jnp.maximum(x, 0.0) + jnp.log1p(jnp.exp(-jnp.abs(x)))


def _silu(x):
    return x * jax.nn.sigmoid(x)


def _split3(x):
    hi = x.astype(BF16)
    r = x - hi.astype(F32)
    mid = r.astype(BF16)
    lo = (r - mid.astype(F32)).astype(BF16)
    return hi, mid, lo


def _dot(a, b):
    return jnp.dot(a, b, preferred_element_type=F32)


def _dot_nt(a, b):
    return lax.dot_general(a, b, (((1,), (1,)), ((), ())), preferred_element_type=F32)


def _dot3_rhs(m, x):
    hi, mid, lo = _split3(x)
    return _dot(m, hi) + _dot(m, mid) + _dot(m, lo)


def _norm_mm_kernel(x_ref, g_ref, w_ref, o_ref, xn_ref, *, act):
    @pl.when(pl.program_id(1) == 0)
    def _():
        xn_ref[...] = _rms(x_ref[...], g_ref[...]).astype(BF16)

    w = w_ref[...]
    chunk = xn_ref.shape[0] // ROW_CHUNKS
    for r in range(ROW_CHUNKS):
        rows = slice(r * chunk, (r + 1) * chunk)
        acc = _dot(xn_ref[rows, :], w)
        if act == "gelu":
            acc = 0.5 * acc * (1.0 + lax.erf(acc * np.float32(np.sqrt(0.5))))
        o_ref[rows, :] = acc.astype(o_ref.dtype)


def _norm_matmul(x, g, w, layer, *, out_dtype, act=None, tm, tn, n=None):
    t, d = x.shape
    n = w.shape[2] if n is None else n
    return pl.pallas_call(
        functools.partial(_norm_mm_kernel, act=act),
        grid=(t // tm, n // tn),
        in_specs=[
            pl.BlockSpec((tm, d), lambda i, j: (i, 0)),
            pl.BlockSpec((1, d), lambda i, j: (0, 0)),
            pl.BlockSpec((None, d, tn), lambda i, j: (layer, 0, j)),
        ],
        out_specs=pl.BlockSpec((tm, tn), lambda i, j: (i, j)),
        out_shape=jax.ShapeDtypeStruct((t, n), out_dtype),
        scratch_shapes=[pltpu.VMEM((tm, d), BF16)],
        compiler_params=_cparams("parallel", "arbitrary"),
        name="norm_matmul",
    )(x, g.reshape(1, d), w)


def _qkv_kernel(x_ref, g_ref, w_ref, hg_ref, o_ref, xn_ref, *, tiles_per_part):
    j = pl.program_id(1)

    @pl.when(j == 0)
    def _():
        xn_ref[...] = _rms(x_ref[...], g_ref[...]).astype(BF16)

    is_v = j >= 2 * tiles_per_part
    gain = hg_ref[...]
    w = w_ref[...]
    half = xn_ref.shape[0] // ROW_CHUNKS
    for r in range(ROW_CHUNKS):
        rows = slice(r * half, (r + 1) * half)
        acc = _dot(xn_ref[rows, :], w)
        for c in range(acc.shape[1] // SB_HEAD_DIM):
            cols = slice(c * SB_HEAD_DIM, (c + 1) * SB_HEAD_DIM)
            blk = acc[:, cols]
            ms = jnp.mean(blk * blk, axis=-1, keepdims=True)
            factor = jnp.where(is_v, 1.0, lax.rsqrt(ms + EPS))
            o_ref[rows, cols] = (blk * factor * gain).astype(o_ref.dtype)


def _qkv_proj(x, g, w, layer, q_g, k_g, *, tm, tn):
    t, d = x.shape
    n = w.shape[2]
    tiles_per_part = n // 3 // tn
    head_gains = jnp.stack([q_g, k_g, jnp.ones_like(q_g)]).reshape(3, 1, SB_HEAD_DIM)
    return pl.pallas_call(
        functools.partial(_qkv_kernel, tiles_per_part=tiles_per_part),
        grid=(t // tm, n // tn),
        in_specs=[
            pl.BlockSpec((tm, d), lambda i, j: (i, 0)),
            pl.BlockSpec((1, d), lambda i, j: (0, 0)),
            pl.BlockSpec((None, d, tn), lambda i, j: (layer, 0, j)),
            pl.BlockSpec((None, 1, SB_HEAD_DIM), lambda i, j: (j // tiles_per_part, 0, 0)),
        ],
        out_specs=pl.BlockSpec((tm, tn), lambda i, j: (i, j)),
        out_shape=jax.ShapeDtypeStruct((t, n), BF16),
        scratch_shapes=[pltpu.VMEM((tm, d), BF16)],
        compiler_params=_cparams("parallel", "arbitrary"),
        name="qkv_proj",
    )(x, g.reshape(1, d), w, head_gains)


def _sb_attn_kernel(q_ref, k_ref, v_ref, u_ref, w1_ref, w2_ref, o_ref, w1b_ref, w2b_ref,
                    lb_ref, l1m_ref, *, tq, hp, rs, cast_blocks):
    step = (pl.program_id(0) * pl.num_programs(1) + pl.program_id(1)) * pl.num_programs(2) \
        + pl.program_id(2)

    @pl.when(step < cast_blocks)
    def _():
        w1b_ref[...] = w1_ref[...].astype(BF16)

    @pl.when(jnp.logical_and(step >= cast_blocks, step < 2 * cast_blocks))
    def _():
        w2b_ref[...] = w2_ref[...].astype(BF16)

    qi = pl.program_id(2)
    d = SB_HEAD_DIM
    upper = u_ref[...]
    ns = tq // rs

    def logits_stage(j, slot, diagonal):
        start = pl.multiple_of(j * tq, tq)
        for h in range(hp):
            kb = k_ref[pl.ds(start, tq), h * d:(h + 1) * d]
            for s in range(ns):
                rows = slice(s * rs, (s + 1) * rs)
                z = _dot_nt(q_ref[rows, h * d:(h + 1) * d], kb)
                lse = jnp.log(1.0 + jnp.exp2(-jnp.abs(z))) * LOG2E
                log_beta = jnp.minimum(z, 0.0) - lse
                log_1m = log_beta - z
                if diagonal:
                    row = lax.broadcasted_iota(jnp.int32, (rs, tq), 0) + s * rs
                    col = lax.broadcasted_iota(jnp.int32, (rs, tq), 1)
                    causal = col < row
                    log_1m = jnp.where(causal, log_1m, 0.0)
                    log_beta = jnp.where(causal, log_beta, -jnp.inf)
                lb_ref[slot, h, rows, :] = log_beta
                l1m_ref[slot, h * tq + s * rs:h * tq + (s + 1) * rs, :] = log_1m.astype(BF16)

    def weights_stage(j, slot, carries, accs):
        start = pl.multiple_of(j * tq, tq)
        new_carries, new_accs = [], []
        for h in range(hp):
            vb = v_ref[pl.ds(start, tq), h * d:(h + 1) * d]
            for s in range(ns):
                i = h * ns + s
                rows = slice(s * rs, (s + 1) * rs)
                l1m = l1m_ref[slot, h * tq + s * rs:h * tq + (s + 1) * rs, :]
                suffix = _dot(l1m, upper)
                a = jnp.exp2(lb_ref[slot, h, rows, :] + suffix + carries[i])
                new_accs.append(accs[i] + _dot(a.astype(BF16), vb))
                new_carries.append(carries[i] + suffix[:, 0:1] + l1m[:, 0:1].astype(F32))
        return tuple(new_carries), tuple(new_accs)

    carries = tuple(jnp.zeros((rs, 1), F32) for _ in range(hp * ns))
    accs = tuple(jnp.zeros((rs, d), F32) for _ in range(hp * ns))
    def run_tiles(j, count, first_slot, c):
        for u in range(count):
            logits_stage(j - u - 1, (first_slot + u + 1) % 2, False)
            c = weights_stage(j - u, (first_slot + u) % 2, c[0], c[1])
        return c

    carry = (carries, accs)
    logits_stage(qi, qi % 2, True)
    carry = lax.cond(qi % 2 == 1, lambda c: run_tiles(qi, 1, 1, c), lambda c: c, carry)
    done = qi % 2
    block = 2
    while block < UNROLL:
        j = qi - done
        carry = lax.cond((qi // block) % 2 == 1,
                         functools.partial(run_tiles, j, block, 0), lambda c: c, carry)
        done = done + ((qi // block) % 2) * block
        block *= 2
    top = qi - qi % UNROLL

    def body(step, c):
        return run_tiles(top - UNROLL * step, UNROLL, 0, c)

    carry = lax.fori_loop(0, top // UNROLL, body, carry)
    carries, accs = weights_stage(0, 0, carry[0], carry[1])
    for h in range(hp):
        for s in range(ns):
            o_ref[s * rs:(s + 1) * rs, h * d:(h + 1) * d] = accs[h * ns + s].astype(o_ref.dtype)


def _sb_attention(qkv, batch, seq, w1, w2, first_layer, n_layers, *, tq, hp, rs):
    t, n = qkv.shape
    heads = n // (3 * SB_HEAD_DIM)
    hg = heads // hp
    nq = seq // tq
    w = hp * SB_HEAD_DIM
    s_idx = np.arange(tq)[:, None]
    j_idx = np.arange(tq)[None, :]
    upper = jnp.asarray((s_idx > j_idx).astype(np.float32), dtype=BF16)

    _, r1, c1 = w1.shape
    _, r2, c2 = w2.shape
    cast_blocks = n_layers * CAST_SPLIT
    assert 2 * cast_blocks <= batch * hg * nq
    step = lambda b, h, i: (b * hg + h) * nq + i
    blk1 = lambda b, h, i: jnp.minimum(step(b, h, i), cast_blocks - 1)
    blk2 = lambda b, h, i: jnp.clip(step(b, h, i) - cast_blocks, 0, cast_blocks - 1)
    return pl.pallas_call(
        functools.partial(_sb_attn_kernel, tq=tq, hp=hp, rs=rs, cast_blocks=cast_blocks),
        grid=(batch, hg, nq),
        in_specs=[
            pl.BlockSpec((tq, w), lambda b, h, i: (b * nq + i, h)),
            pl.BlockSpec((seq, w), lambda b, h, i: (b, hg + h)),
            pl.BlockSpec((seq, w), lambda b, h, i: (b, 2 * hg + h)),
            pl.BlockSpec((tq, tq), lambda b, h, i: (0, 0)),
            pl.BlockSpec((None, r1 // CAST_SPLIT, c1),
                         lambda b, h, i: (first_layer + blk1(b, h, i) // CAST_SPLIT,
                                          blk1(b, h, i) % CAST_SPLIT, 0)),
            pl.BlockSpec((None, r2 // CAST_SPLIT, c2),
                         lambda b, h, i: (first_layer + blk2(b, h, i) // CAST_SPLIT,
                                          blk2(b, h, i) % CAST_SPLIT, 0)),
        ],
        out_specs=[
            pl.BlockSpec((tq, w), lambda b, h, i: (b * nq + i, h)),
            pl.BlockSpec((None, r1 // CAST_SPLIT, c1),
                         lambda b, h, i: (blk1(b, h, i) // CAST_SPLIT, blk1(b, h, i) % CAST_SPLIT, 0)),
            pl.BlockSpec((None, r2 // CAST_SPLIT, c2),
                         lambda b, h, i: (blk2(b, h, i) // CAST_SPLIT, blk2(b, h, i) % CAST_SPLIT, 0)),
        ],
        out_shape=[
            jax.ShapeDtypeStruct((t, heads * SB_HEAD_DIM), BF16),
            jax.ShapeDtypeStruct((n_layers, r1, c1), BF16),
            jax.ShapeDtypeStruct((n_layers, r2, c2), BF16),
        ],
        scratch_shapes=[pltpu.VMEM((2, hp, tq, tq), F32), pltpu.VMEM((2, hp * tq, tq), BF16)],
        compiler_params=pltpu.CompilerParams(
            dimension_semantics=("arbitrary", "arbitrary", "arbitrary"),
            vmem_limit_bytes=VMEM_LIMIT_BYTES),
        name="sb_attention",
    )(qkv, qkv, qkv, upper, w1, w2)


def _mm_res_kernel(a_ref, w_ref, r_ref, o_ref):
    o_ref[...] = r_ref[...] + _dot(a_ref[...], w_ref[...])


def _matmul_residual(a, w, layer, res, *, tm, tn):
    t, k = a.shape
    n = w.shape[2]
    return pl.pallas_call(
        _mm_res_kernel,
        grid=(t // tm, n // tn),
        in_specs=[
            pl.BlockSpec((tm, k), lambda i, j: (i, 0)),
            pl.BlockSpec((None, k, tn), lambda i, j: (layer, 0, j)),
            pl.BlockSpec((tm, tn), lambda i, j: (i, j)),
        ],
        out_specs=pl.BlockSpec((tm, tn), lambda i, j: (i, j)),
        out_shape=jax.ShapeDtypeStruct((t, n), F32),
        compiler_params=_cparams("parallel", "arbitrary"),
        name="matmul_residual",
    )(a, w, res)


def _mlp_kernel(x_ref, g_ref, w1_ref, w2_ref, o_ref, xn_ref):
    @pl.when(pl.program_id(1) == 0)
    def _():
        x = x_ref[...]
        xn_ref[...] = _rms(x, g_ref[...]).astype(BF16)
        o_ref[...] = x

    h = _dot(xn_ref[...], w1_ref[...])
    h = jnp.square(jnp.maximum(h, 0.0))
    o_ref[...] += _dot(h.astype(BF16), w2_ref[...])


def _mlp(x, g, w1, w2, layer, *, tm, th):
    t, d = x.shape
    hidden = w1.shape[2]
    return pl.pallas_call(
        _mlp_kernel,
        grid=(t // tm, hidden // th),
        in_specs=[
            pl.BlockSpec((tm, d), lambda i, k: (i, 0)),
            pl.BlockSpec((1, d), lambda i, k: (0, 0)),
            pl.BlockSpec((None, d, th), lambda i, k: (layer, 0, k)),
            pl.BlockSpec((None, th, d), lambda i, k: (layer, k, 0)),
        ],
        out_specs=pl.BlockSpec((tm, d), lambda i, k: (i, 0)),
        out_shape=jax.ShapeDtypeStruct((t, d), F32),
        scratch_shapes=[pltpu.VMEM((tm, d), BF16)],
        compiler_params=_cparams("parallel", "arbitrary"),
        name="relu2_mlp",
    )(x, g.reshape(1, d), w1, w2)


def _gm_gate_kernel(u_ref, v_ref, vg_ref, ws_ref, bst_ref, x_ref, wo_ref, o_ref, vn_ref, y_ref):
    tm = u_ref.shape[0]
    groups = ws_ref.shape[0]
    vn_ref[...] = _rms(v_ref[...].astype(F32), vg_ref[...]).astype(BF16)
    row = lax.broadcasted_iota(jnp.int32, (GM_CHUNK, GM_CHUNK), 0)
    col = lax.broadcasted_iota(jnp.int32, (GM_CHUNK, GM_CHUNK), 1)
    causal = col <= row
    for g in range(groups):
        cols = slice(g * GM_GROUP_DIM, (g + 1) * GM_GROUP_DIM)
        w = jnp.where(causal, ws_ref[g], 0.0).astype(BF16)
        bias = bst_ref[:, g:g + 1]
        for c in range(tm // GM_CHUNK):
            rows = slice(c * GM_CHUNK, (c + 1) * GM_CHUNK)
            mixed = _dot(w, vn_ref[rows, cols]) + bias
            y_ref[rows, cols] = (u_ref[rows, cols].astype(F32) * mixed).astype(BF16)
    o_ref[...] = x_ref[...] + _dot(y_ref[...], wo_ref[...])


def _gm_gate(uv, v_g, w_s, b_s, x, w_o, layer, *, tm):
    t, d = x.shape
    width = uv.shape[1] // 2
    groups = w_s.shape[0]
    return pl.pallas_call(
        _gm_gate_kernel,
        grid=(t // tm,),
        in_specs=[
            pl.BlockSpec((tm, width), lambda i: (i, 0)),
            pl.BlockSpec((tm, width), lambda i: (i, 1)),
            pl.BlockSpec((1, width), lambda i: (0, 0)),
            pl.BlockSpec((groups, GM_CHUNK, GM_CHUNK), lambda i: (0, 0, 0)),
            pl.BlockSpec((GM_CHUNK, groups), lambda i: (0, 0)),
            pl.BlockSpec((tm, d), lambda i: (i, 0)),
            pl.BlockSpec((None, width, d), lambda i: (layer, 0, 0)),
        ],
        out_specs=pl.BlockSpec((tm, d), lambda i: (i, 0)),
        out_shape=jax.ShapeDtypeStruct((t, d), F32),
        scratch_shapes=[pltpu.VMEM((tm, width), BF16), pltpu.VMEM((tm, width), BF16)],
        compiler_params=_cparams("parallel"),
        name="gmlp_gate",
    )(uv, uv, v_g.reshape(1, width), w_s, b_s.T, x, w_o)


def _causal_conv_silu(ext_ref, cols, w, b):
    rows = ext_ref.shape[0] - SUBLANE
    acc = None
    for s in range(SSD_CONV):
        wk = w[SSD_CONV - 1 - s:SSD_CONV - s, :]
        term = ext_ref[pl.ds(SUBLANE - s, rows), cols] * wk
        acc = term if acc is None else acc + term
    return _silu(acc + b)


def _ssd_kernel(z_ref, xi_ref, bm_ref, cm_ref, dt_ref,
                wx_ref, wb_ref, wc_ref, bx_ref, bb_ref, bc_ref,
                dtb_ref, alog_ref, e_ref, dskip_ref, ng_ref, tri_ref,
                o_ref,
                state_ref, tx_ref, tb_ref, tc_ref):
    L = SSD_CHUNK
    gw = state_ref.shape[2]
    hpg = gw // SSD_HEAD_DIM
    n = SSD_STATE

    top = slice(0, SUBLANE)
    cur = slice(SUBLANE, SUBLANE + L)

    @pl.when(pl.program_id(1) == 0)
    def _():
        state_ref[...] = jnp.zeros(state_ref.shape, F32)
        tx_ref[top, :] = jnp.zeros((SUBLANE, tx_ref.shape[1]), F32)
        tb_ref[top, :] = jnp.zeros((SUBLANE, tb_ref.shape[1]), F32)
        tc_ref[top, :] = jnp.zeros((SUBLANE, tc_ref.shape[1]), F32)

    tx_ref[cur, :] = xi_ref[...].astype(F32)
    tb_ref[cur, :] = bm_ref[...].astype(F32)
    tc_ref[cur, :] = cm_ref[...].astype(F32)

    dt = _softplus(dt_ref[...] + dtb_ref[...])
    a = -jnp.exp(alog_ref[...])
    a_cum = _dot3_rhs(tri_ref[...], dt * a)
    a_cum_t = a_cum.T
    dt_parts = _split3(dt)
    ac_parts = _split3(a_cum)

    def expand_heads(parts, sel):
        return _dot(parts[0], sel) + _dot(parts[1], sel) + _dot(parts[2], sel)

    row = lax.broadcasted_iota(jnp.int32, (L, L), 0)
    col = lax.broadcasted_iota(jnp.int32, (L, L), 1)
    tri = col <= row
    lane = lax.broadcasted_iota(jnp.int32, (L, 2 * SSD_HEAD_DIM), 1)
    first = lane < SSD_HEAD_DIM

    for g in range(state_ref.shape[0]):
        gx = slice(g * gw, (g + 1) * gw)
        gn = slice(g * n, (g + 1) * n)
        xi = _causal_conv_silu(tx_ref, gx, wx_ref[:, gx], bx_ref[:, gx])
        bm = _causal_conv_silu(tb_ref, gn, wb_ref[:, gn], bb_ref[:, gn])
        cm = _causal_conv_silu(tc_ref, gn, wc_ref[:, gn], bc_ref[:, gn])

        expand = e_ref[:, gx]
        dt_x = expand_heads(dt_parts, expand)
        ac_x = expand_heads(ac_parts, expand)
        last_x = ac_x[L - 1:L, :]
        xs = xi * dt_x
        xsd = xs * jnp.exp(last_x - ac_x)

        cm_b = cm.astype(BF16)
        bm_b = bm.astype(BF16)
        xs_b = xs.astype(BF16)
        state = state_ref[g]
        y = _dot(cm_b, state.astype(BF16)) * jnp.exp(ac_x)
        state_ref[g] = jnp.exp(last_x) * state + _dot(bm.T.astype(BF16), xsd.astype(BF16))

        cb = _dot_nt(cm_b, bm_b)
        y_diag = []
        for pair in range(hpg // 2):
            outs = []
            for r in (2 * pair, 2 * pair + 1):
                head = g * hpg + r
                a_col = ac_x[:, r * SSD_HEAD_DIM:r * SSD_HEAD_DIM + 1]
                a_row = a_cum_t[head:head + 1, :]
                decay = jnp.exp(jnp.where(tri, a_col - a_row, -jnp.inf))
                m = (cb * decay).astype(BF16)
                outs.append(_dot(m, xs_b[:, pair * 2 * SSD_HEAD_DIM:(pair + 1) * 2 * SSD_HEAD_DIM]))
            y_diag.append(jnp.where(first, outs[0], outs[1]))
        y = y + jnp.concatenate(y_diag, axis=1)
        y = y + dskip_ref[:, gx] * xi
        y = y * _silu(z_ref[:, gx].astype(F32))
        o_ref[:, gx] = _rms(y, ng_ref[:, gx]).astype(o_ref.dtype)

    last = slice(L, L + SUBLANE)
    tx_ref[top, :] = tx_ref[last, :]
    tb_ref[top, :] = tb_ref[last, :]
    tc_ref[top, :] = tc_ref[last, :]


def _ssd_scan(zxbc, dt_raw, conv_w, conv_b, dt_bias, a_log, d_skip, norm_g, batch, seq):
    t = zxbc.shape[0]
    heads = d_skip.shape[0]
    inner = heads * SSD_HEAD_DIM
    gw = inner // SSD_GROUPS
    gn = SSD_GROUPS * SSD_STATE
    nc = seq // SSD_CHUNK
    hp = dt_raw.shape[1]
    L = SSD_CHUNK

    pad = hp - heads
    dtb = jnp.pad(dt_bias, (0, pad)).reshape(1, hp)
    alog = jnp.pad(a_log, (0, pad)).reshape(1, hp)
    e = np.zeros((hp, inner), np.float32)
    for h in range(heads):
        e[h, h * SSD_HEAD_DIM:(h + 1) * SSD_HEAD_DIM] = 1.0
    e = jnp.asarray(e, dtype=BF16)
    tri = jnp.asarray(np.tril(np.ones((L, L), np.float32)), dtype=BF16)
    dskip_x = jnp.repeat(d_skip, SSD_HEAD_DIM).reshape(1, inner)
    conv_b2 = conv_b.reshape(1, -1)

    row_blk = lambda b, c: b * nc + c
    const = lambda b, c: (0, 0)
    in_specs = [
        pl.BlockSpec((L, inner), lambda b, c: (row_blk(b, c), 0)),
        pl.BlockSpec((L, inner), lambda b, c: (row_blk(b, c), 1)),
        pl.BlockSpec((L, gn), lambda b, c: (row_blk(b, c), 2 * inner // gn)),
        pl.BlockSpec((L, gn), lambda b, c: (row_blk(b, c), 2 * inner // gn + 1)),
        pl.BlockSpec((L, hp), lambda b, c: (row_blk(b, c), 0)),
        pl.BlockSpec((SSD_CONV, inner), const),
        pl.BlockSpec((SSD_CONV, gn), lambda b, c: (0, inner // gn)),
        pl.BlockSpec((SSD_CONV, gn), lambda b, c: (0, inner // gn + 1)),
        pl.BlockSpec((1, inner), const),
        pl.BlockSpec((1, gn), lambda b, c: (0, inner // gn)),
        pl.BlockSpec((1, gn), lambda b, c: (0, inner // gn + 1)),
        pl.BlockSpec((1, hp), const),
        pl.BlockSpec((1, hp), const),
        pl.BlockSpec((hp, inner), const),
        pl.BlockSpec((1, inner), const),
        pl.BlockSpec((1, inner), const),
        pl.BlockSpec((L, L), const),
    ]
    return pl.pallas_call(
        _ssd_kernel,
        grid=(batch, nc),
        in_specs=in_specs,
        out_specs=pl.BlockSpec((L, inner), lambda b, c: (row_blk(b, c), 0)),
        out_shape=jax.ShapeDtypeStruct((t, inner), BF16),
        scratch_shapes=[
            pltpu.VMEM((SSD_GROUPS, SSD_STATE, gw), F32),
            pltpu.VMEM((SUBLANE + L, inner), F32),
            pltpu.VMEM((SUBLANE + L, gn), F32),
            pltpu.VMEM((SUBLANE + L, gn), F32),
        ],
        compiler_params=_cparams("arbitrary", "arbitrary"),
        name="ssd_scan",
    )(zxbc, zxbc, zxbc, zxbc, dt_raw,
      conv_w, conv_w, conv_w, conv_b2, conv_b2, conv_b2,
      dtb, alog, e, dskip_x, norm_g.reshape(1, inner), tri)


def _sb_layer(h, norm_g, w_qkv, q_g, k_g, w_o, j, batch, seq, mlp_w1, mlp_w2, mlp_first, mlp_count):
    q_gain = q_g * np.float32(SB_HEAD_DIM ** -0.5 * LOG2E)
    qkv = _qkv_proj(h, norm_g, w_qkv, j, q_gain, k_g, tm=1024, tn=1024)
    o, w1b, w2b = _sb_attention(qkv, batch, seq, mlp_w1, mlp_w2, mlp_first, mlp_count,
                                tq=256, hp=4, rs=256)
    h = _matmul_residual(o, w_o, j, h, tm=512, tn=w_o.shape[2])
    return h, w1b, w2b


def _gm_layer(h, norm_g, w_in, v_g, w_s, b_s, w_o, j):
    uv = _norm_matmul(h, norm_g, w_in, j, out_dtype=BF16, act="gelu", tm=1024, tn=1024)
    return _gm_gate(uv, v_g, w_s, b_s, h, w_o, j, tm=256)


def _ssd_layer(h, norm_g, w_in, w_in_f32, conv_w, conv_b, dt_bias, a_log, d_skip, ssd_norm_g, w_o, j,
               batch, seq):
    heads = d_skip.shape[0]
    main = w_in.shape[2] - heads
    hp = -(-heads // LANE) * LANE
    w_dt = jnp.pad(w_in_f32[j, :, main:], ((0, 0), (0, hp - heads))).astype(BF16)[None]
    zxbc = _norm_matmul(h, norm_g, w_in, j, out_dtype=BF16, tm=1024, tn=1024, n=main)
    dt_raw = _norm_matmul(h, norm_g, w_dt, 0, out_dtype=F32, tm=1024, tn=hp)
    y = _ssd_scan(zxbc, dt_raw, conv_w, conv_b, dt_bias, a_log, d_skip, ssd_norm_g, batch, seq)
    return _matmul_residual(y, w_o, j, h, tm=1024, tn=512)


def kernel(x, norm_mix_g, norm_mlp_g, sb_w_qkv, sb_q_norm_g, sb_k_norm_g, sb_w_o, gm_w_in, gm_v_norm_g, gm_w_s, gm_b_s, gm_w_o, ssd_w_in, ssd_conv_w, ssd_conv_b, ssd_dt_bias, ssd_a_log, ssd_d, ssd_norm_g, ssd_w_o, mlp_w_in, mlp_w_out):
    batch, seq, d = x.shape
    depth = norm_mix_g.shape[0]
    sb_w_qkv_b, sb_w_o_b = sb_w_qkv.astype(BF16), sb_w_o.astype(BF16)
    gm_w_in_b, gm_w_o_b = gm_w_in.astype(BF16), gm_w_o.astype(BF16)
    ssd_w_in_b, ssd_w_o_b = ssd_w_in.astype(BF16), ssd_w_o.astype(BF16)
    h = x.reshape(batch * seq, d)
    for i in range(depth):
        kind, j = i % N_MIXERS, i // N_MIXERS
        if kind == 0:
            mlp_first, mlp_count = i, min(N_MIXERS, depth - i)
            h, mlp_w_in_b, mlp_w_out_b = _sb_layer(
                h, norm_mix_g[i], sb_w_qkv_b, sb_q_norm_g[j], sb_k_norm_g[j], sb_w_o_b, j, batch, seq,
                mlp_w_in, mlp_w_out, mlp_first, mlp_count)
        elif kind == 1:
            h = _gm_layer(h, norm_mix_g[i], gm_w_in_b, gm_v_norm_g[j], gm_w_s[j], gm_b_s[j], gm_w_o_b, j)
        else:
            h = _ssd_layer(h, norm_mix_g[i], ssd_w_in_b, ssd_w_in, ssd_conv_w[j], ssd_conv_b[j],
                           ssd_dt_bias[j], ssd_a_log[j], ssd_d[j], ssd_norm_g[j], ssd_w_o_b, j, batch, seq)
        h = _mlp(h, norm_mlp_g[i], mlp_w_in_b, mlp_w_out_b, i - mlp_first, tm=512, th=1024)
    return h.reshape(batch, seq, d)
```

```python
import functools

import jax
import jax.numpy as jnp
import numpy as np
from jax import lax
from jax.experimental import pallas as pl
from jax.experimental.pallas import tpu as pltpu

F32 = jnp.float32
BF16 = jnp.bfloat16

EPS = 1e-6
LOG2E = float(np.log2(np.e))
LANE = 128
SUBLANE = 8
N_MIXERS = 3

SB_HEAD_DIM = 128
GM_CHUNK = 128
GM_GROUP_DIM = 128
SSD_HEAD_DIM = 64
SSD_GROUPS = 8
SSD_STATE = 128
SSD_CONV = 4
SSD_CHUNK = 128

VMEM_LIMIT_BYTES = 48 * 1024 * 1024
ATTN_VMEM_LIMIT_BYTES = 56 * 1024 * 1024
ROW_CHUNKS = 4
UNROLL = 4
CAST_SPLIT = 16
Q_TILES = 2


def _cparams(*sem):
    return pltpu.CompilerParams(dimension_semantics=sem, vmem_limit_bytes=VMEM_LIMIT_BYTES)


def _rms(x, g):
    ms = jnp.mean(x * x, axis=-1, keepdims=True)
    return x * lax.rsqrt(ms + EPS) * g


def _softplus(x):
    return jnp.maximum(x, 0.0) + jnp.log1p(jnp.exp(-jnp.abs(x)))


def _silu(x):
    return x * jax.nn.sigmoid(x)


def _split3(x):
    hi = x.astype(BF16)
    r = x - hi.astype(F32)
    mid = r.astype(BF16)
    lo = (r - mid.astype(F32)).astype(BF16)
    return hi, mid, lo


def _dot(a, b):
    return jnp.dot(a, b, preferred_element_type=F32)


def _dot_nt(a, b):
    return lax.dot_general(a, b, (((1,), (1,)), ((), ())), preferred_element_type=F32)


def _dot3_rhs(m, x):
    hi, mid, lo = _split3(x)
    return _dot(m, hi) + _dot(m, mid) + _dot(m, lo)


def _norm_mm_kernel(x_ref, g_ref, w_ref, o_ref, xn_ref, *, act):
    @pl.when(pl.program_id(1) == 0)
    def _():
        xn_ref[...] = _rms(x_ref[...], g_ref[...]).astype(BF16)

    w = w_ref[...]
    chunk = xn_ref.shape[0] // ROW_CHUNKS
    for r in range(ROW_CHUNKS):
        rows = slice(r * chunk, (r + 1) * chunk)
        acc = _dot(xn_ref[rows, :], w)
        if act == "gelu":
            acc = 0.5 * acc * (1.0 + lax.erf(acc * np.float32(np.sqrt(0.5))))
        o_ref[rows, :] = acc.astype(o_ref.dtype)


def _norm_matmul(x, g, w, layer, *, out_dtype, act=None, tm, tn, n=None):
    t, d = x.shape
    n = w.shape[2] if n is None else n
    return pl.pallas_call(
        functools.partial(_norm_mm_kernel, act=act),
        grid=(t // tm, n // tn),
        in_specs=[
            pl.BlockSpec((tm, d), lambda i, j: (i, 0)),
            pl.BlockSpec((1, d), lambda i, j: (0, 0)),
            pl.BlockSpec((None, d, tn), lambda i, j: (layer, 0, j)),
        ],
        out_specs=pl.BlockSpec((tm, tn), lambda i, j: (i, j)),
        out_shape=jax.ShapeDtypeStruct((t, n), out_dtype),
        scratch_shapes=[pltpu.VMEM((tm, d), BF16)],
        compiler_params=_cparams("parallel", "arbitrary"),
        name="norm_matmul",
    )(x, g.reshape(1, d), w)


def _qkv_kernel(x_ref, g_ref, w_ref, hg_ref, o_ref, xn_ref, *, tiles_per_part):
    j = pl.program_id(1)

    @pl.when(j == 0)
    def _():
        xn_ref[...] = _rms(x_ref[...], g_ref[...]).astype(BF16)

    is_v = j >= 2 * tiles_per_part
    gain = hg_ref[...]
    w = w_ref[...]
    half = xn_ref.shape[0] // ROW_CHUNKS
    for r in range(ROW_CHUNKS):
        rows = slice(r * half, (r + 1) * half)
        acc = _dot(xn_ref[rows, :], w)
        for c in range(acc.shape[1] // SB_HEAD_DIM):
            cols = slice(c * SB_HEAD_DIM, (c + 1) * SB_HEAD_DIM)
            blk = acc[:, cols]
            ms = jnp.mean(blk * blk, axis=-1, keepdims=True)
            factor = jnp.where(is_v, 1.0, lax.rsqrt(ms + EPS))
            o_ref[rows, cols] = (blk * factor * gain).astype(o_ref.dtype)


def _qkv_proj(x, g, w, layer, q_g, k_g, *, tm, tn):
    t, d = x.shape
    n = w.shape[2]
    tiles_per_part = n // 3 // tn
    head_gains = jnp.stack([q_g, k_g, jnp.ones_like(q_g)]).reshape(3, 1, SB_HEAD_DIM)
    return pl.pallas_call(
        functools.partial(_qkv_kernel, tiles_per_part=tiles_per_part),
        grid=(t // tm, n // tn),
        in_specs=[
            pl.BlockSpec((tm, d), lambda i, j: (i, 0)),
            pl.BlockSpec((1, d), lambda i, j: (0, 0)),
            pl.BlockSpec((None, d, tn), lambda i, j: (layer, 0, j)),
            pl.BlockSpec((None, 1, SB_HEAD_DIM), lambda i, j: (j // tiles_per_part, 0, 0)),
        ],
        out_specs=pl.BlockSpec((tm, tn), lambda i, j: (i, j)),
        out_shape=jax.ShapeDtypeStruct((t, n), BF16),
        scratch_shapes=[pltpu.VMEM((tm, d), BF16)],
        compiler_params=_cparams("parallel", "arbitrary"),
        name="qkv_proj",
    )(x, g.reshape(1, d), w, head_gains)


def _sb_attn_kernel(q_ref, k_ref, v_ref, u_ref, w1_ref, w2_ref, o_ref, w1b_ref, w2b_ref,
                    lb_ref, l1m_ref, *, tq, hp, cast_blocks):
    step = (pl.program_id(0) * pl.num_programs(1) + pl.program_id(1)) * pl.num_programs(2) \
        + pl.program_id(2)

    @pl.when(step < cast_blocks)
    def _():
        w1b_ref[...] = w1_ref[...].astype(BF16)
        w2b_ref[...] = w2_ref[...].astype(BF16)

    upper = u_ref[...]
    for sub in range(Q_TILES):
        _sb_attn_q_tile(q_ref, k_ref, v_ref, upper, o_ref, lb_ref, l1m_ref,
                        pl.program_id(2) * Q_TILES + sub, sub * tq, tq=tq, hp=hp)


def _sb_attn_q_tile(q_ref, k_ref, v_ref, upper, o_ref, lb_ref, l1m_ref, qi, row0, *, tq, hp):
    d = SB_HEAD_DIM
    parity = (row0 // tq) % 2
    q_rows = slice(row0, row0 + tq)

    def logits_stage(j, slot, diagonal):
        start = pl.multiple_of(j * tq, tq)
        if diagonal:
            row = lax.broadcasted_iota(jnp.int32, (tq, tq), 0)
            col = lax.broadcasted_iota(jnp.int32, (tq, tq), 1)
            causal = col < row
        for h in range(hp):
            z = _dot_nt(q_ref[q_rows, h * d:(h + 1) * d], k_ref[pl.ds(start, tq), h * d:(h + 1) * d])
            lse = jnp.log(1.0 + jnp.exp2(-jnp.abs(z))) * LOG2E
            log_beta = jnp.minimum(z, 0.0) - lse
            log_1m = log_beta - z
            if diagonal:
                log_1m = jnp.where(causal, log_1m, 0.0)
                log_beta = jnp.where(causal, log_beta, -jnp.inf)
            lb_ref[slot, h] = log_beta
            l1m_ref[slot, h] = log_1m.astype(BF16)

    def weights_stage(j, slot, carries, accs):
        start = pl.multiple_of(j * tq, tq)
        new_carries, new_accs = [], []
        for h in range(hp):
            l1m = l1m_ref[slot, h]
            suffix = _dot(l1m, upper)
            a = jnp.exp2(lb_ref[slot, h] + suffix + carries[h])
            vb = v_ref[pl.ds(start, tq), h * d:(h + 1) * d]
            new_accs.append(accs[h] + _dot(a.astype(BF16), vb))
            new_carries.append(carries[h] + suffix[:, 0:1] + l1m[:, 0:1].astype(F32))
        return tuple(new_carries), tuple(new_accs)

    carries = tuple(jnp.zeros((tq, 1), F32) for _ in range(hp))
    accs = tuple(jnp.zeros((tq, d), F32) for _ in range(hp))

    def run_tiles(j, count, first_slot, c):
        for u in range(count):
            logits_stage(j - u - 1, (first_slot + u + 1) % 2, False)
            c = weights_stage(j - u, (first_slot + u) % 2, c[0], c[1])
        return c

    carry = (carries, accs)
    logits_stage(qi, parity, True)
    if parity == 1:
        carry = run_tiles(qi, 1, 1, carry)
    done = parity
    block = 2
    while block < UNROLL:
        j = qi - done
        carry = lax.cond((qi // block) % 2 == 1,
                         functools.partial(run_tiles, j, block, 0), lambda c: c, carry)
        done = done + ((qi // block) % 2) * block
        block *= 2
    top = qi - qi % UNROLL

    def body(step, c):
        return run_tiles(top - UNROLL * step, UNROLL, 0, c)

    carry = lax.fori_loop(0, top // UNROLL, body, carry)
    carries, accs = weights_stage(0, 0, carry[0], carry[1])
    for h in range(hp):
        o_ref[q_rows, h * d:(h + 1) * d] = accs[h].astype(o_ref.dtype)


def _sb_attention(qkv, batch, seq, w1, w2, first_layer, n_layers, *, tq, hp):
    t, n = qkv.shape
    heads = n // (3 * SB_HEAD_DIM)
    hg = heads // hp
    nq = seq // tq
    w = hp * SB_HEAD_DIM
    s_idx = np.arange(tq)[:, None]
    j_idx = np.arange(tq)[None, :]
    upper = jnp.asarray((s_idx > j_idx).astype(np.float32), dtype=BF16)

    _, r1, c1 = w1.shape
    _, r2, c2 = w2.shape
    cast_blocks = n_layers * CAST_SPLIT
    nqs = nq // Q_TILES
    assert cast_blocks <= batch * hg * nqs
    blk1 = blk2 = lambda b, h, i: jnp.minimum((b * hg + h) * nqs + i, cast_blocks - 1)
    return pl.pallas_call(
        functools.partial(_sb_attn_kernel, tq=tq, hp=hp, cast_blocks=cast_blocks),
        grid=(batch, hg, nqs),
        in_specs=[
            pl.BlockSpec((Q_TILES * tq, w), lambda b, h, i: (b * nqs + i, h)),
            pl.BlockSpec((seq, w), lambda b, h, i: (b, hg + h)),
            pl.BlockSpec((seq, w), lambda b, h, i: (b, 2 * hg + h)),
            pl.BlockSpec((tq, tq), lambda b, h, i: (0, 0)),
            pl.BlockSpec((None, r1 // CAST_SPLIT, c1),
                         lambda b, h, i: (first_layer + blk1(b, h, i) // CAST_SPLIT,
                                          blk1(b, h, i) % CAST_SPLIT, 0)),
            pl.BlockSpec((None, r2 // CAST_SPLIT, c2),
                         lambda b, h, i: (first_layer + blk2(b, h, i) // CAST_SPLIT,
                                          blk2(b, h, i) % CAST_SPLIT, 0)),
        ],
        out_specs=[
            pl.BlockSpec((Q_TILES * tq, w), lambda b, h, i: (b * nqs + i, h)),
            pl.BlockSpec((None, r1 // CAST_SPLIT, c1),
                         lambda b, h, i: (blk1(b, h, i) // CAST_SPLIT, blk1(b, h, i) % CAST_SPLIT, 0)),
            pl.BlockSpec((None, r2 // CAST_SPLIT, c2),
                         lambda b, h, i: (blk2(b, h, i) // CAST_SPLIT, blk2(b, h, i) % CAST_SPLIT, 0)),
        ],
        out_shape=[
            jax.ShapeDtypeStruct((t, heads * SB_HEAD_DIM), BF16),
            jax.ShapeDtypeStruct((n_layers, r1, c1), BF16),
            jax.ShapeDtypeStruct((n_layers, r2, c2), BF16),
        ],
        scratch_shapes=[pltpu.VMEM((2, hp, tq, tq), F32), pltpu.VMEM((2, hp, tq, tq), BF16)],
        compiler_params=pltpu.CompilerParams(
            dimension_semantics=("arbitrary", "arbitrary", "arbitrary"),
            vmem_limit_bytes=ATTN_VMEM_LIMIT_BYTES),
        name="sb_attention",
    )(qkv, qkv, qkv, upper, w1, w2)


def _mm_res_kernel(a_ref, w_ref, r_ref, o_ref):
    o_ref[...] = r_ref[...] + _dot(a_ref[...], w_ref[...])


def _matmul_residual(a, w, layer, res, *, tm, tn):
    t, k = a.shape
    n = w.shape[2]
    return pl.pallas_call(
        _mm_res_kernel,
        grid=(t // tm, n // tn),
        in_specs=[
            pl.BlockSpec((tm, k), lambda i, j: (i, 0)),
            pl.BlockSpec((None, k, tn), lambda i, j: (layer, 0, j)),
            pl.BlockSpec((tm, tn), lambda i, j: (i, j)),
        ],
        out_specs=pl.BlockSpec((tm, tn), lambda i, j: (i, j)),
        out_shape=jax.ShapeDtypeStruct((t, n), F32),
        compiler_params=_cparams("parallel", "arbitrary"),
        name="matmul_residual",
    )(a, w, res)


def _mlp_kernel(x_ref, g_ref, w1_ref, w2_ref, o_ref, xn_ref):
    @pl.when(pl.program_id(1) == 0)
    def _():
        x = x_ref[...]
        xn_ref[...] = _rms(x, g_ref[...]).astype(BF16)
        o_ref[...] = x

    h = _dot(xn_ref[...], w1_ref[...])
    h = jnp.square(jnp.maximum(h, 0.0))
    o_ref[...] += _dot(h.astype(BF16), w2_ref[...])


def _mlp(x, g, w1, w2, layer, *, tm, th):
    t, d = x.shape
    hidden = w1.shape[2]
    return pl.pallas_call(
        _mlp_kernel,
        grid=(t // tm, hidden // th),
        in_specs=[
            pl.BlockSpec((tm, d), lambda i, k: (i, 0)),
            pl.BlockSpec((1, d), lambda i, k: (0, 0)),
            pl.BlockSpec((None, d, th), lambda i, k: (layer, 0, k)),
            pl.BlockSpec((None, th, d), lambda i, k: (layer, k, 0)),
        ],
        out_specs=pl.BlockSpec((tm, d), lambda i, k: (i, 0)),
        out_shape=jax.ShapeDtypeStruct((t, d), F32),
        scratch_shapes=[pltpu.VMEM((tm, d), BF16)],
        compiler_params=_cparams("parallel", "arbitrary"),
        name="relu2_mlp",
    )(x, g.reshape(1, d), w1, w2)


def _gm_gate_kernel(u_ref, v_ref, vg_ref, ws_ref, bst_ref, x_ref, wo_ref, o_ref, vn_ref, y_ref):
    tm = u_ref.shape[0]
    groups = ws_ref.shape[0]
    vn_ref[...] = _rms(v_ref[...].astype(F32), vg_ref[...]).astype(BF16)
    row = lax.broadcasted_iota(jnp.int32, (GM_CHUNK, GM_CHUNK), 0)
    col = lax.broadcasted_iota(jnp.int32, (GM_CHUNK, GM_CHUNK), 1)
    causal = col <= row
    for g in range(groups):
        cols = slice(g * GM_GROUP_DIM, (g + 1) * GM_GROUP_DIM)
        w = jnp.where(causal, ws_ref[g], 0.0).astype(BF16)
        bias = bst_ref[:, g:g + 1]
        for c in range(tm // GM_CHUNK):
            rows = slice(c * GM_CHUNK, (c + 1) * GM_CHUNK)
            mixed = _dot(w, vn_ref[rows, cols]) + bias
            y_ref[rows, cols] = (u_ref[rows, cols].astype(F32) * mixed).astype(BF16)
    o_ref[...] = x_ref[...] + _dot(y_ref[...], wo_ref[...])


def _gm_gate(uv, v_g, w_s, b_s, x, w_o, layer, *, tm):
    t, d = x.shape
    width = uv.shape[1] // 2
    groups = w_s.shape[0]
    return pl.pallas_call(
        _gm_gate_kernel,
        grid=(t // tm,),
        in_specs=[
            pl.BlockSpec((tm, width), lambda i: (i, 0)),
            pl.BlockSpec((tm, width), lambda i: (i, 1)),
            pl.BlockSpec((1, width), lambda i: (0, 0)),
            pl.BlockSpec((groups, GM_CHUNK, GM_CHUNK), lambda i: (0, 0, 0)),
            pl.BlockSpec((GM_CHUNK, groups), lambda i: (0, 0)),
            pl.BlockSpec((tm, d), lambda i: (i, 0)),
            pl.BlockSpec((None, width, d), lambda i: (layer, 0, 0)),
        ],
        out_specs=pl.BlockSpec((tm, d), lambda i: (i, 0)),
        out_shape=jax.ShapeDtypeStruct((t, d), F32),
        scratch_shapes=[pltpu.VMEM((tm, width), BF16), pltpu.VMEM((tm, width), BF16)],
        compiler_params=_cparams("parallel"),
        name="gmlp_gate",
    )(uv, uv, v_g.reshape(1, width), w_s, b_s.T, x, w_o)


def _causal_conv_silu(ext_ref, cols, w, b):
    rows = ext_ref.shape[0] - SUBLANE
    acc = None
    for s in range(SSD_CONV):
        wk = w[SSD_CONV - 1 - s:SSD_CONV - s, :]
        term = ext_ref[pl.ds(SUBLANE - s, rows), cols] * wk
        acc = term if acc is None else acc + term
    return _silu(acc + b)


def _ssd_kernel(z_ref, xi_ref, bm_ref, cm_ref, dt_ref,
                wx_ref, wb_ref, wc_ref, bx_ref, bb_ref, bc_ref,
                dtb_ref, alog_ref, e_ref, dskip_ref, ng_ref, tri_ref,
                o_ref,
                state_ref, tx_ref, tb_ref, tc_ref):
    L = SSD_CHUNK
    gw = state_ref.shape[2]
    hpg = gw // SSD_HEAD_DIM
    n = SSD_STATE

    top = slice(0, SUBLANE)
    cur = slice(SUBLANE, SUBLANE + L)

    @pl.when(pl.program_id(1) == 0)
    def _():
        state_ref[...] = jnp.zeros(state_ref.shape, F32)
        tx_ref[top, :] = jnp.zeros((SUBLANE, tx_ref.shape[1]), F32)
        tb_ref[top, :] = jnp.zeros((SUBLANE, tb_ref.shape[1]), F32)
        tc_ref[top, :] = jnp.zeros((SUBLANE, tc_ref.shape[1]), F32)

    tx_ref[cur, :] = xi_ref[...].astype(F32)
    tb_ref[cur, :] = bm_ref[...].astype(F32)
    tc_ref[cur, :] = cm_ref[...].astype(F32)

    dt = _softplus(dt_ref[...] + dtb_ref[...])
    a = -jnp.exp(alog_ref[...])
    a_cum = _dot3_rhs(tri_ref[...], dt * a)
    a_cum_t = a_cum.T
    dt_parts = _split3(dt)
    ac_parts = _split3(a_cum)

    def expand_heads(parts, sel):
        return _dot(parts[0], sel) + _dot(parts[1], sel) + _dot(parts[2], sel)

    row = lax.broadcasted_iota(jnp.int32, (L, L), 0)
    col = lax.broadcasted_iota(jnp.int32, (L, L), 1)
    tri = col <= row
    lane = lax.broadcasted_iota(jnp.int32, (L, 2 * SSD_HEAD_DIM), 1)
    first = lane < SSD_HEAD_DIM

    for g in range(state_ref.shape[0]):
        gx = slice(g * gw, (g + 1) * gw)
        gn = slice(g * n, (g + 1) * n)
        xi = _causal_conv_silu(tx_ref, gx, wx_ref[:, gx], bx_ref[:, gx])
        bm = _causal_conv_silu(tb_ref, gn, wb_ref[:, gn], bb_ref[:, gn])
        cm = _causal_conv_silu(tc_ref, gn, wc_ref[:, gn], bc_ref[:, gn])

        expand = e_ref[:, gx]
        dt_x = expand_heads(dt_parts, expand)
        ac_x = expand_heads(ac_parts, expand)
        last_x = ac_x[L - 1:L, :]
        xs = xi * dt_x
        xsd = xs * jnp.exp(last_x - ac_x)

        cm_b = cm.astype(BF16)
        bm_b = bm.astype(BF16)
        xs_b = xs.astype(BF16)
        state = state_ref[g]
        y = _dot(cm_b, state.astype(BF16)) * jnp.exp(ac_x)
        state_ref[g] = jnp.exp(last_x) * state + _dot(bm.T.astype(BF16), xsd.astype(BF16))

        cb = _dot_nt(cm_b, bm_b)
        y_diag = []
        for pair in range(hpg // 2):
            outs = []
            for r in (2 * pair, 2 * pair + 1):
                head = g * hpg + r
                a_col = ac_x[:, r * SSD_HEAD_DIM:r * SSD_HEAD_DIM + 1]
                a_row = a_cum_t[head:head + 1, :]
                decay = jnp.exp(jnp.where(tri, a_col - a_row, -jnp.inf))
                m = (cb * decay).astype(BF16)
                outs.append(_dot(m, xs_b[:, pair * 2 * SSD_HEAD_DIM:(pair + 1) * 2 * SSD_HEAD_DIM]))
            y_diag.append(jnp.where(first, outs[0], outs[1]))
        y = y + jnp.concatenate(y_diag, axis=1)
        y = y + dskip_ref[:, gx] * xi
        y = y * _silu(z_ref[:, gx].astype(F32))
        o_ref[:, gx] = _rms(y, ng_ref[:, gx]).astype(o_ref.dtype)

    last = slice(L, L + SUBLANE)
    tx_ref[top, :] = tx_ref[last, :]
    tb_ref[top, :] = tb_ref[last, :]
    tc_ref[top, :] = tc_ref[last, :]


def _ssd_scan(zxbc, dt_raw, conv_w, conv_b, dt_bias, a_log, d_skip, norm_g, batch, seq):
    t = zxbc.shape[0]
    heads = d_skip.shape[0]
    inner = heads * SSD_HEAD_DIM
    gw = inner // SSD_GROUPS
    gn = SSD_GROUPS * SSD_STATE
    nc = seq // SSD_CHUNK
    hp = dt_raw.shape[1]
    L = SSD_CHUNK

    pad = hp - heads
    dtb = jnp.pad(dt_bias, (0, pad)).reshape(1, hp)
    alog = jnp.pad(a_log, (0, pad)).reshape(1, hp)
    e = np.zeros((hp, inner), np.float32)
    for h in range(heads):
        e[h, h * SSD_HEAD_DIM:(h + 1) * SSD_HEAD_DIM] = 1.0
    e = jnp.asarray(e, dtype=BF16)
    tri = jnp.asarray(np.tril(np.ones((L, L), np.float32)), dtype=BF16)
    dskip_x = jnp.repeat(d_skip, SSD_HEAD_DIM).reshape(1, inner)
    conv_b2 = conv_b.reshape(1, -1)

    row_blk = lambda b, c: b * nc + c
    const = lambda b, c: (0, 0)
    in_specs = [
        pl.BlockSpec((L, inner), lambda b, c: (row_blk(b, c), 0)),
        pl.BlockSpec((L, inner), lambda b, c: (row_blk(b, c), 1)),
        pl.BlockSpec((L, gn), lambda b, c: (row_blk(b, c), 2 * inner // gn)),
        pl.BlockSpec((L, gn), lambda b, c: (row_blk(b, c), 2 * inner // gn + 1)),
        pl.BlockSpec((L, hp), lambda b, c: (row_blk(b, c), 0)),
        pl.BlockSpec((SSD_CONV, inner), const),
        pl.BlockSpec((SSD_CONV, gn), lambda b, c: (0, inner // gn)),
        pl.BlockSpec((SSD_CONV, gn), lambda b, c: (0, inner // gn + 1)),
        pl.BlockSpec((1, inner), const),
        pl.BlockSpec((1, gn), lambda b, c: (0, inner // gn)),
        pl.BlockSpec((1, gn), lambda b, c: (0, inner // gn + 1)),
        pl.BlockSpec((1, hp), const),
        pl.BlockSpec((1, hp), const),
        pl.BlockSpec((hp, inner), const),
        pl.BlockSpec((1, inner), const),
        pl.BlockSpec((1, inner), const),
        pl.BlockSpec((L, L), const),
    ]
    return pl.pallas_call(
        _ssd_kernel,
        grid=(batch, nc),
        in_specs=in_specs,
        out_specs=pl.BlockSpec((L, inner), lambda b, c: (row_blk(b, c), 0)),
        out_shape=jax.ShapeDtypeStruct((t, inner), BF16),
        scratch_shapes=[
            pltpu.VMEM((SSD_GROUPS, SSD_STATE, gw), F32),
            pltpu.VMEM((SUBLANE + L, inner), F32),
            pltpu.VMEM((SUBLANE + L, gn), F32),
            pltpu.VMEM((SUBLANE + L, gn), F32),
        ],
        compiler_params=_cparams("arbitrary", "arbitrary"),
        name="ssd_scan",
    )(zxbc, zxbc, zxbc, zxbc, dt_raw,
      conv_w, conv_w, conv_w, conv_b2, conv_b2, conv_b2,
      dtb, alog, e, dskip_x, norm_g.reshape(1, inner), tri)


def _sb_layer(h, norm_g, w_qkv, q_g, k_g, w_o, j, batch, seq, mlp_w1, mlp_w2, mlp_first, mlp_count):
    q_gain = q_g * np.float32(SB_HEAD_DIM ** -0.5 * LOG2E)
    qkv = _qkv_proj(h, norm_g, w_qkv, j, q_gain, k_g, tm=1024, tn=1024)
    o, w1b, w2b = _sb_attention(qkv, batch, seq, mlp_w1, mlp_w2, mlp_first, mlp_count,
                                tq=256, hp=4)
    h = _matmul_residual(o, w_o, j, h, tm=512, tn=w_o.shape[2])
    return h, w1b, w2b


def _gm_layer(h, norm_g, w_in, v_g, w_s, b_s, w_o, j):
    uv = _norm_matmul(h, norm_g, w_in, j, out_dtype=BF16, act="gelu", tm=1024, tn=1024)
    return _gm_gate(uv, v_g, w_s, b_s, h, w_o, j, tm=256)


def _ssd_layer(h, norm_g, w_in, w_in_f32, conv_w, conv_b, dt_bias, a_log, d_skip, ssd_norm_g, w_o, j,
               batch, seq):
    heads = d_skip.shape[0]
    main = w_in.shape[2] - heads
    hp = -(-heads // LANE) * LANE
    w_dt = jnp.pad(w_in_f32[j, :, main:], ((0, 0), (0, hp - heads))).astype(BF16)[None]
    zxbc = _norm_matmul(h, norm_g, w_in, j, out_dtype=BF16, tm=1024, tn=1024, n=main)
    dt_raw = _norm_matmul(h, norm_g, w_dt, 0, out_dtype=F32, tm=1024, tn=hp)
    y = _ssd_scan(zxbc, dt_raw, conv_w, conv_b, dt_bias, a_log, d_skip, ssd_norm_g, batch, seq)
    return _matmul_residual(y, w_o, j, h, tm=1024, tn=512)


def kernel(x, norm_mix_g, norm_mlp_g, sb_w_qkv, sb_q_norm_g, sb_k_norm_g, sb_w_o, gm_w_in, gm_v_norm_g, gm_w_s, gm_b_s, gm_w_o, ssd_w_in, ssd_conv_w, ssd_conv_b, ssd_dt_bias, ssd_a_log, ssd_d, ssd_norm_g, ssd_w_o, mlp_w_in, mlp_w_out):
    batch, seq, d = x.shape
    depth = norm_mix_g.shape[0]
    sb_w_qkv_b, sb_w_o_b = sb_w_qkv.astype(BF16), sb_w_o.astype(BF16)
    gm_w_in_b, gm_w_o_b = gm_w_in.astype(BF16), gm_w_o.astype(BF16)
    ssd_w_in_b, ssd_w_o_b = ssd_w_in.astype(BF16), ssd_w_o.astype(BF16)
    h = x.reshape(batch * seq, d)
    for i in range(depth):
        kind, j = i % N_MIXERS, i // N_MIXERS
        if kind == 0:
            mlp_first, mlp_count = i, min(N_MIXERS, depth - i)
            h, mlp_w_in_b, mlp_w_out_b = _sb_layer(
                h, norm_mix_g[i], sb_w_qkv_b, sb_q_norm_g[j], sb_k_norm_g[j], sb_w_o_b, j, batch, seq,
                mlp_w_in, mlp_w_out, mlp_first, mlp_count)
        elif kind == 1:
            h = _gm_layer(h, norm_mix_g[i], gm_w_in_b, gm_v_norm_g[j], gm_w_s[j], gm_b_s[j], gm_w_o_b, j)
        else:
            h = _ssd_layer(h, norm_mix_g[i], ssd_w_in_b, ssd_w_in, ssd_conv_w[j], ssd_conv_b[j],
                           ssd_dt_bias[j], ssd_a_log[j], ssd_d[j], ssd_norm_g[j], ssd_w_o_b, j, batch, seq)
        h = _mlp(h, norm_mlp_g[i], mlp_w_in_b, mlp_w_out_b, i - mlp_first, tm=512, th=1024)
    return h.reshape(batch, seq, d)
```

```python
import functools

import jax
import jax.numpy as jnp
import numpy as np
from jax import lax
from jax.experimental import pallas as pl
from jax.experimental.pallas import tpu as pltpu

F32 = jnp.float32
BF16 = jnp.bfloat16

EPS = 1e-6
LOG2E = float(np.log2(np.e))
LANE = 128
SUBLANE = 8
N_MIXERS = 3

SB_HEAD_DIM = 128
GM_CHUNK = 128
GM_GROUP_DIM = 128
SSD_HEAD_DIM = 64
SSD_GROUPS = 8
SSD_STATE = 128
SSD_CONV = 4
SSD_CHUNK = 128

VMEM_LIMIT_BYTES = 48 * 1024 * 1024
ATTN_VMEM_LIMIT_BYTES = 56 * 1024 * 1024
ROW_CHUNKS = 4
UNROLL = 4
CAST_SPLIT = 16
Q_TILES = 4


def _cparams(*sem):
    return pltpu.CompilerParams(dimension_semantics=sem, vmem_limit_bytes=VMEM_LIMIT_BYTES)


def _rms(x, g):
    ms = jnp.mean(x * x, axis=-1, keepdims=True)
    return x * lax.rsqrt(ms + EPS) * g


def _softplus(x):
    return jnp.maximum(x, 0.0) + jnp.log1p(jnp.exp(-jnp.abs(x)))


def _silu(x):
    return x * jax.nn.sigmoid(x)


def _split3(x):
    hi = x.astype(BF16)
    r = x - hi.astype(F32)
    mid = r.astype(BF16)
    lo = (r - mid.astype(F32)).astype(BF16)
    return hi, mid, lo


def _dot(a, b):
    return jnp.dot(a, b, preferred_element_type=F32)


def _dot_nt(a, b):
    return lax.dot_general(a, b, (((1,), (1,)), ((), ())), preferred_element_type=F32)


def _dot3_rhs(m, x):
    hi, mid, lo = _split3(x)
    return _dot(m, hi) + _dot(m, mid) + _dot(m, lo)


def _norm_mm_kernel(x_ref, g_ref, w_ref, o_ref, xn_ref, *, act):
    @pl.when(pl.program_id(1) == 0)
    def _():
        xn_ref[...] = _rms(x_ref[...], g_ref[...]).astype(BF16)

    w = w_ref[...]
    chunk = xn_ref.shape[0] // ROW_CHUNKS
    for r in range(ROW_CHUNKS):
        rows = slice(r * chunk, (r + 1) * chunk)
        acc = _dot(xn_ref[rows, :], w)
        if act == "gelu":
            acc = 0.5 * acc * (1.0 + lax.erf(acc * np.float32(np.sqrt(0.5))))
        o_ref[rows, :] = acc.astype(o_ref.dtype)


def _norm_matmul(x, g, w, layer, *, out_dtype, act=None, tm, tn, n=None):
    t, d = x.shape
    n = w.shape[2] if n is None else n
    return pl.pallas_call(
        functools.partial(_norm_mm_kernel, act=act),
        grid=(t // tm, n // tn),
        in_specs=[
            pl.BlockSpec((tm, d), lambda i, j: (i, 0)),
            pl.BlockSpec((1, d), lambda i, j: (0, 0)),
            pl.BlockSpec((None, d, tn), lambda i, j: (layer, 0, j)),
        ],
        out_specs=pl.BlockSpec((tm, tn), lambda i, j: (i, j)),
        out_shape=jax.ShapeDtypeStruct((t, n), out_dtype),
        scratch_shapes=[pltpu.VMEM((tm, d), BF16)],
        compiler_params=_cparams("parallel", "arbitrary"),
        name="norm_matmul",
    )(x, g.reshape(1, d), w)


def _qkv_kernel(x_ref, g_ref, w_ref, hg_ref, o_ref, xn_ref, *, tiles_per_part):
    j = pl.program_id(1)

    @pl.when(j == 0)
    def _():
        xn_ref[...] = _rms(x_ref[...], g_ref[...]).astype(BF16)

    is_v = j >= 2 * tiles_per_part
    gain = hg_ref[...]
    w = w_ref[...]
    half = xn_ref.shape[0] // ROW_CHUNKS
    for r in range(ROW_CHUNKS):
        rows = slice(r * half, (r + 1) * half)
        acc = _dot(xn_ref[rows, :], w)
        for c in range(acc.shape[1] // SB_HEAD_DIM):
            cols = slice(c * SB_HEAD_DIM, (c + 1) * SB_HEAD_DIM)
            blk = acc[:, cols]
            ms = jnp.mean(blk * blk, axis=-1, keepdims=True)
            factor = jnp.where(is_v, 1.0, lax.rsqrt(ms + EPS))
            o_ref[rows, cols] = (blk * factor * gain).astype(o_ref.dtype)


def _qkv_proj(x, g, w, layer, q_g, k_g, *, tm, tn):
    t, d = x.shape
    n = w.shape[2]
    tiles_per_part = n // 3 // tn
    head_gains = jnp.stack([q_g, k_g, jnp.ones_like(q_g)]).reshape(3, 1, SB_HEAD_DIM)
    return pl.pallas_call(
        functools.partial(_qkv_kernel, tiles_per_part=tiles_per_part),
        grid=(t // tm, n // tn),
        in_specs=[
            pl.BlockSpec((tm, d), lambda i, j: (i, 0)),
            pl.BlockSpec((1, d), lambda i, j: (0, 0)),
            pl.BlockSpec((None, d, tn), lambda i, j: (layer, 0, j)),
            pl.BlockSpec((None, 1, SB_HEAD_DIM), lambda i, j: (j // tiles_per_part, 0, 0)),
        ],
        out_specs=pl.BlockSpec((tm, tn), lambda i, j: (i, j)),
        out_shape=jax.ShapeDtypeStruct((t, n), BF16),
        scratch_shapes=[pltpu.VMEM((tm, d), BF16)],
        compiler_params=_cparams("parallel", "arbitrary"),
        name="qkv_proj",
    )(x, g.reshape(1, d), w, head_gains)


def _cast_specs(w1, w2, first_layer, n_layers, n_steps, step_of):
    cast_blocks = n_layers * CAST_SPLIT
    assert cast_blocks <= n_steps
    blk = lambda *ids: jnp.minimum(step_of(*ids), cast_blocks - 1)
    in_specs, out_specs, out_shapes = [], [], []
    for wgt in (w1, w2):
        _, r, c = wgt.shape
        block = (None, r // CAST_SPLIT, c)
        in_specs.append(pl.BlockSpec(
            block, lambda *ids: (first_layer + blk(*ids) // CAST_SPLIT, blk(*ids) % CAST_SPLIT, 0)))
        out_specs.append(pl.BlockSpec(
            block, lambda *ids: (blk(*ids) // CAST_SPLIT, blk(*ids) % CAST_SPLIT, 0)))
        out_shapes.append(jax.ShapeDtypeStruct((n_layers, r, c), BF16))
    return in_specs, out_specs, out_shapes, cast_blocks


def _cast_side_job(step, cast_blocks, w1_ref, w2_ref, w1b_ref, w2b_ref):
    @pl.when(step < cast_blocks)
    def _():
        w1b_ref[...] = w1_ref[...].astype(BF16)
        w2b_ref[...] = w2_ref[...].astype(BF16)


def _sb_attn_kernel(q_ref, k_ref, v_ref, u_ref, w1_ref, w2_ref, o_ref, w1b_ref, w2b_ref,
                    lb_ref, l1m_ref, *, tq, hp, cast_blocks):
    step = (pl.program_id(0) * pl.num_programs(1) + pl.program_id(1)) * pl.num_programs(2) \
        + pl.program_id(2)
    _cast_side_job(step, cast_blocks, w1_ref, w2_ref, w1b_ref, w2b_ref)

    upper = u_ref[...]
    for sub in range(Q_TILES):
        _sb_attn_q_tile(q_ref, k_ref, v_ref, upper, o_ref, lb_ref, l1m_ref,
                        pl.program_id(2) * Q_TILES + sub, sub * tq, tq=tq, hp=hp)


def _sb_attn_q_tile(q_ref, k_ref, v_ref, upper, o_ref, lb_ref, l1m_ref, qi, row0, *, tq, hp):
    d = SB_HEAD_DIM
    parity = (row0 // tq) % 2
    q_rows = slice(row0, row0 + tq)

    def logits_stage(j, slot, diagonal):
        start = pl.multiple_of(j * tq, tq)
        if diagonal:
            row = lax.broadcasted_iota(jnp.int32, (tq, tq), 0)
            col = lax.broadcasted_iota(jnp.int32, (tq, tq), 1)
            causal = col < row
        for h in range(hp):
            z = _dot_nt(q_ref[q_rows, h * d:(h + 1) * d], k_ref[pl.ds(start, tq), h * d:(h + 1) * d])
            lse = jnp.log(1.0 + jnp.exp2(-jnp.abs(z))) * LOG2E
            log_beta = jnp.minimum(z, 0.0) - lse
            log_1m = log_beta - z
            if diagonal:
                log_1m = jnp.where(causal, log_1m, 0.0)
                log_beta = jnp.where(causal, log_beta, -jnp.inf)
            lb_ref[slot, h] = log_beta
            l1m_ref[slot, h] = log_1m.astype(BF16)

    def weights_stage(j, slot, carries, accs):
        start = pl.multiple_of(j * tq, tq)
        new_carries, new_accs = [], []
        for h in range(hp):
            l1m = l1m_ref[slot, h]
            suffix = _dot(l1m, upper)
            a = jnp.exp2(lb_ref[slot, h] + suffix + carries[h])
            vb = v_ref[pl.ds(start, tq), h * d:(h + 1) * d]
            new_accs.append(accs[h] + _dot(a.astype(BF16), vb))
            new_carries.append(carries[h] + suffix[:, 0:1] + l1m[:, 0:1].astype(F32))
        return tuple(new_carries), tuple(new_accs)

    carries = tuple(jnp.zeros((tq, 1), F32) for _ in range(hp))
    accs = tuple(jnp.zeros((tq, d), F32) for _ in range(hp))

    def run_tiles(j, count, first_slot, c):
        for u in range(count):
            logits_stage(j - u - 1, (first_slot + u + 1) % 2, False)
            c = weights_stage(j - u, (first_slot + u) % 2, c[0], c[1])
        return c

    carry = (carries, accs)
    logits_stage(qi, parity, True)
    if parity == 1:
        carry = run_tiles(qi, 1, 1, carry)
    done = parity
    block = 2
    while block < UNROLL:
        j = qi - done
        carry = lax.cond((qi // block) % 2 == 1,
                         functools.partial(run_tiles, j, block, 0), lambda c: c, carry)
        done = done + ((qi // block) % 2) * block
        block *= 2
    top = qi - qi % UNROLL

    def body(step, c):
        return run_tiles(top - UNROLL * step, UNROLL, 0, c)

    carry = lax.fori_loop(0, top // UNROLL, body, carry)
    carries, accs = weights_stage(0, 0, carry[0], carry[1])
    for h in range(hp):
        o_ref[q_rows, h * d:(h + 1) * d] = accs[h].astype(o_ref.dtype)


def _sb_attention(qkv, batch, seq, w1, w2, first_layer, n_layers, *, tq, hp):
    t, n = qkv.shape
    heads = n // (3 * SB_HEAD_DIM)
    hg = heads // hp
    nq = seq // tq
    w = hp * SB_HEAD_DIM
    s_idx = np.arange(tq)[:, None]
    j_idx = np.arange(tq)[None, :]
    upper = jnp.asarray((s_idx > j_idx).astype(np.float32), dtype=BF16)

    nqs = nq // Q_TILES
    cast_in, cast_out, cast_shapes, cast_blocks = _cast_specs(
        w1, w2, first_layer, n_layers, batch * hg * nqs, lambda b, h, i: (b * hg + h) * nqs + i)
    return pl.pallas_call(
        functools.partial(_sb_attn_kernel, tq=tq, hp=hp, cast_blocks=cast_blocks),
        grid=(batch, hg, nqs),
        in_specs=[
            pl.BlockSpec((Q_TILES * tq, w), lambda b, h, i: (b * nqs + i, h)),
            pl.BlockSpec((seq, w), lambda b, h, i: (b, hg + h)),
            pl.BlockSpec((seq, w), lambda b, h, i: (b, 2 * hg + h)),
            pl.BlockSpec((tq, tq), lambda b, h, i: (0, 0)),
        ] + cast_in,
        out_specs=[pl.BlockSpec((Q_TILES * tq, w), lambda b, h, i: (b * nqs + i, h))] + cast_out,
        out_shape=[jax.ShapeDtypeStruct((t, heads * SB_HEAD_DIM), BF16)] + cast_shapes,
        scratch_shapes=[pltpu.VMEM((2, hp, tq, tq), F32), pltpu.VMEM((2, hp, tq, tq), BF16)],
        compiler_params=pltpu.CompilerParams(
            dimension_semantics=("arbitrary", "arbitrary", "arbitrary"),
            vmem_limit_bytes=ATTN_VMEM_LIMIT_BYTES),
        name="sb_attention",
    )(qkv, qkv, qkv, upper, w1, w2)


def _mm_res_kernel(a_ref, w_ref, r_ref, o_ref):
    o_ref[...] = r_ref[...] + _dot(a_ref[...], w_ref[...])


def _matmul_residual(a, w, layer, res, *, tm, tn):
    t, k = a.shape
    n = w.shape[2]
    return pl.pallas_call(
        _mm_res_kernel,
        grid=(t // tm, n // tn),
        in_specs=[
            pl.BlockSpec((tm, k), lambda i, j: (i, 0)),
            pl.BlockSpec((None, k, tn), lambda i, j: (layer, 0, j)),
            pl.BlockSpec((tm, tn), lambda i, j: (i, j)),
        ],
        out_specs=pl.BlockSpec((tm, tn), lambda i, j: (i, j)),
        out_shape=jax.ShapeDtypeStruct((t, n), F32),
        compiler_params=_cparams("parallel", "arbitrary"),
        name="matmul_residual",
    )(a, w, res)


def _mlp_kernel(x_ref, g_ref, w1_ref, w2_ref, o_ref, xn_ref):
    @pl.when(pl.program_id(1) == 0)
    def _():
        x = x_ref[...]
        xn_ref[...] = _rms(x, g_ref[...]).astype(BF16)
        o_ref[...] = x

    h = _dot(xn_ref[...], w1_ref[...])
    h = jnp.square(jnp.maximum(h, 0.0))
    o_ref[...] += _dot(h.astype(BF16), w2_ref[...])


def _mlp(x, g, w1, w2, layer, *, tm, th):
    t, d = x.shape
    hidden = w1.shape[2]
    return pl.pallas_call(
        _mlp_kernel,
        grid=(t // tm, hidden // th),
        in_specs=[
            pl.BlockSpec((tm, d), lambda i, k: (i, 0)),
            pl.BlockSpec((1, d), lambda i, k: (0, 0)),
            pl.BlockSpec((None, d, th), lambda i, k: (layer, 0, k)),
            pl.BlockSpec((None, th, d), lambda i, k: (layer, k, 0)),
        ],
        out_specs=pl.BlockSpec((tm, d), lambda i, k: (i, 0)),
        out_shape=jax.ShapeDtypeStruct((t, d), F32),
        scratch_shapes=[pltpu.VMEM((tm, d), BF16)],
        compiler_params=_cparams("parallel", "arbitrary"),
        name="relu2_mlp",
    )(x, g.reshape(1, d), w1, w2)


def _gm_gate_kernel(u_ref, v_ref, vg_ref, ws_ref, bst_ref, x_ref, wo_ref, o_ref, vn_ref, y_ref):
    tm = u_ref.shape[0]
    groups = ws_ref.shape[0]
    vn_ref[...] = _rms(v_ref[...].astype(F32), vg_ref[...]).astype(BF16)
    row = lax.broadcasted_iota(jnp.int32, (GM_CHUNK, GM_CHUNK), 0)
    col = lax.broadcasted_iota(jnp.int32, (GM_CHUNK, GM_CHUNK), 1)
    causal = col <= row
    for g in range(groups):
        cols = slice(g * GM_GROUP_DIM, (g + 1) * GM_GROUP_DIM)
        w = jnp.where(causal, ws_ref[g], 0.0).astype(BF16)
        bias = bst_ref[:, g:g + 1]
        for c in range(tm // GM_CHUNK):
            rows = slice(c * GM_CHUNK, (c + 1) * GM_CHUNK)
            mixed = _dot(w, vn_ref[rows, cols]) + bias
            y_ref[rows, cols] = (u_ref[rows, cols].astype(F32) * mixed).astype(BF16)
    o_ref[...] = x_ref[...] + _dot(y_ref[...], wo_ref[...])


def _gm_gate(uv, v_g, w_s, b_s, x, w_o, layer, *, tm):
    t, d = x.shape
    width = uv.shape[1] // 2
    groups = w_s.shape[0]
    return pl.pallas_call(
        _gm_gate_kernel,
        grid=(t // tm,),
        in_specs=[
            pl.BlockSpec((tm, width), lambda i: (i, 0)),
            pl.BlockSpec((tm, width), lambda i: (i, 1)),
            pl.BlockSpec((1, width), lambda i: (0, 0)),
            pl.BlockSpec((groups, GM_CHUNK, GM_CHUNK), lambda i: (0, 0, 0)),
            pl.BlockSpec((GM_CHUNK, groups), lambda i: (0, 0)),
            pl.BlockSpec((tm, d), lambda i: (i, 0)),
            pl.BlockSpec((None, width, d), lambda i: (layer, 0, 0)),
        ],
        out_specs=pl.BlockSpec((tm, d), lambda i: (i, 0)),
        out_shape=jax.ShapeDtypeStruct((t, d), F32),
        scratch_shapes=[pltpu.VMEM((tm, width), BF16), pltpu.VMEM((tm, width), BF16)],
        compiler_params=_cparams("parallel"),
        name="gmlp_gate",
    )(uv, uv, v_g.reshape(1, width), w_s, b_s.T, x, w_o)


def _causal_conv_silu(ext_ref, cols, w, b):
    rows = ext_ref.shape[0] - SUBLANE
    acc = None
    for s in range(SSD_CONV):
        wk = w[SSD_CONV - 1 - s:SSD_CONV - s, :]
        term = ext_ref[pl.ds(SUBLANE - s, rows), cols] * wk
        acc = term if acc is None else acc + term
    return _silu(acc + b)


def _ssd_kernel(z_ref, xi_ref, bm_ref, cm_ref, dt_ref,
                wx_ref, wb_ref, wc_ref, bx_ref, bb_ref, bc_ref,
                dtb_ref, alog_ref, e_ref, dskip_ref, ng_ref, tri_ref, w1_ref, w2_ref,
                o_ref, w1b_ref, w2b_ref,
                state_ref, tx_ref, tb_ref, tc_ref, *, cast_blocks):
    _cast_side_job(pl.program_id(0) * pl.num_programs(1) + pl.program_id(1), cast_blocks,
                   w1_ref, w2_ref, w1b_ref, w2b_ref)
    L = SSD_CHUNK
    gw = state_ref.shape[2]
    hpg = gw // SSD_HEAD_DIM
    n = SSD_STATE

    top = slice(0, SUBLANE)
    cur = slice(SUBLANE, SUBLANE + L)

    @pl.when(pl.program_id(1) == 0)
    def _():
        state_ref[...] = jnp.zeros(state_ref.shape, F32)
        tx_ref[top, :] = jnp.zeros((SUBLANE, tx_ref.shape[1]), F32)
        tb_ref[top, :] = jnp.zeros((SUBLANE, tb_ref.shape[1]), F32)
        tc_ref[top, :] = jnp.zeros((SUBLANE, tc_ref.shape[1]), F32)

    tx_ref[cur, :] = xi_ref[...].astype(F32)
    tb_ref[cur, :] = bm_ref[...].astype(F32)
    tc_ref[cur, :] = cm_ref[...].astype(F32)

    dt = _softplus(dt_ref[...] + dtb_ref[...])
    a = -jnp.exp(alog_ref[...])
    a_cum = _dot3_rhs(tri_ref[...], dt * a)
    a_cum_t = a_cum.T
    dt_parts = _split3(dt)
    ac_parts = _split3(a_cum)

    def expand_heads(parts, sel):
        return _dot(parts[0], sel) + _dot(parts[1], sel) + _dot(parts[2], sel)

    row = lax.broadcasted_iota(jnp.int32, (L, L), 0)
    col = lax.broadcasted_iota(jnp.int32, (L, L), 1)
    tri = col <= row
    lane = lax.broadcasted_iota(jnp.int32, (L, 2 * SSD_HEAD_DIM), 1)
    first = lane < SSD_HEAD_DIM

    for g in range(state_ref.shape[0]):
        gx = slice(g * gw, (g + 1) * gw)
        gn = slice(g * n, (g + 1) * n)
        xi = _causal_conv_silu(tx_ref, gx, wx_ref[:, gx], bx_ref[:, gx])
        bm = _causal_conv_silu(tb_ref, gn, wb_ref[:, gn], bb_ref[:, gn])
        cm = _causal_conv_silu(tc_ref, gn, wc_ref[:, gn], bc_ref[:, gn])

        expand = e_ref[:, gx]
        dt_x = expand_heads(dt_parts, expand)
        ac_x = expand_heads(ac_parts, expand)
        last_x = ac_x[L - 1:L, :]
        xs = xi * dt_x
        xsd = xs * jnp.exp(last_x - ac_x)

        cm_b = cm.astype(BF16)
        bm_b = bm.astype(BF16)
        xs_b = xs.astype(BF16)
        state = state_ref[g]
        y = _dot(cm_b, state.astype(BF16)) * jnp.exp(ac_x)
        state_ref[g] = jnp.exp(last_x) * state + _dot(bm.T.astype(BF16), xsd.astype(BF16))

        cb = _dot_nt(cm_b, bm_b)
        y_diag = []
        for pair in range(hpg // 2):
            outs = []
            for r in (2 * pair, 2 * pair + 1):
                head = g * hpg + r
                a_col = ac_x[:, r * SSD_HEAD_DIM:r * SSD_HEAD_DIM + 1]
                a_row = a_cum_t[head:head + 1, :]
                decay = jnp.exp(jnp.where(tri, a_col - a_row, -jnp.inf))
                m = (cb * decay).astype(BF16)
                outs.append(_dot(m, xs_b[:, pair * 2 * SSD_HEAD_DIM:(pair + 1) * 2 * SSD_HEAD_DIM]))
            y_diag.append(jnp.where(first, outs[0], outs[1]))
        y = y + jnp.concatenate(y_diag, axis=1)
        y = y + dskip_ref[:, gx] * xi
        y = y * _silu(z_ref[:, gx].astype(F32))
        o_ref[:, gx] = _rms(y, ng_ref[:, gx]).astype(o_ref.dtype)

    last = slice(L, L + SUBLANE)
    tx_ref[top, :] = tx_ref[last, :]
    tb_ref[top, :] = tb_ref[last, :]
    tc_ref[top, :] = tc_ref[last, :]


def _ssd_scan(zxbc, dt_raw, conv_w, conv_b, dt_bias, a_log, d_skip, norm_g, batch, seq,
              w1, w2, first_layer, n_layers):
    t = zxbc.shape[0]
    heads = d_skip.shape[0]
    inner = heads * SSD_HEAD_DIM
    gw = inner // SSD_GROUPS
    gn = SSD_GROUPS * SSD_STATE
    nc = seq // SSD_CHUNK
    hp = dt_raw.shape[1]
    L = SSD_CHUNK

    pad = hp - heads
    dtb = jnp.pad(dt_bias, (0, pad)).reshape(1, hp)
    alog = jnp.pad(a_log, (0, pad)).reshape(1, hp)
    e = np.zeros((hp, inner), np.float32)
    for h in range(heads):
        e[h, h * SSD_HEAD_DIM:(h + 1) * SSD_HEAD_DIM] = 1.0
    e = jnp.asarray(e, dtype=BF16)
    tri = jnp.asarray(np.tril(np.ones((L, L), np.float32)), dtype=BF16)
    dskip_x = jnp.repeat(d_skip, SSD_HEAD_DIM).reshape(1, inner)
    conv_b2 = conv_b.reshape(1, -1)

    row_blk = lambda b, c: b * nc + c
    const = lambda b, c: (0, 0)
    in_specs = [
        pl.BlockSpec((L, inner), lambda b, c: (row_blk(b, c), 0)),
        pl.BlockSpec((L, inner), lambda b, c: (row_blk(b, c), 1)),
        pl.BlockSpec((L, gn), lambda b, c: (row_blk(b, c), 2 * inner // gn)),
        pl.BlockSpec((L, gn), lambda b, c: (row_blk(b, c), 2 * inner // gn + 1)),
        pl.BlockSpec((L, hp), lambda b, c: (row_blk(b, c), 0)),
        pl.BlockSpec((SSD_CONV, inner), const),
        pl.BlockSpec((SSD_CONV, gn), lambda b, c: (0, inner // gn)),
        pl.BlockSpec((SSD_CONV, gn), lambda b, c: (0, inner // gn + 1)),
        pl.BlockSpec((1, inner), const),
        pl.BlockSpec((1, gn), lambda b, c: (0, inner // gn)),
        pl.BlockSpec((1, gn), lambda b, c: (0, inner // gn + 1)),
        pl.BlockSpec((1, hp), const),
        pl.BlockSpec((1, hp), const),
        pl.BlockSpec((hp, inner), const),
        pl.BlockSpec((1, inner), const),
        pl.BlockSpec((1, inner), const),
        pl.BlockSpec((L, L), const),
    ]
    cast_in, cast_out, cast_shapes, cast_blocks = _cast_specs(
        w1, w2, first_layer, n_layers, batch * nc, row_blk)
    return pl.pallas_call(
        functools.partial(_ssd_kernel, cast_blocks=cast_blocks),
        grid=(batch, nc),
        in_specs=in_specs + cast_in,
        out_specs=[pl.BlockSpec((L, inner), lambda b, c: (row_blk(b, c), 0))] + cast_out,
        out_shape=[jax.ShapeDtypeStruct((t, inner), BF16)] + cast_shapes,
        scratch_shapes=[
            pltpu.VMEM((SSD_GROUPS, SSD_STATE, gw), F32),
            pltpu.VMEM((SUBLANE + L, inner), F32),
            pltpu.VMEM((SUBLANE + L, gn), F32),
            pltpu.VMEM((SUBLANE + L, gn), F32),
        ],
        compiler_params=_cparams("arbitrary", "arbitrary"),
        name="ssd_scan",
    )(zxbc, zxbc, zxbc, zxbc, dt_raw,
      conv_w, conv_w, conv_w, conv_b2, conv_b2, conv_b2,
      dtb, alog, e, dskip_x, norm_g.reshape(1, inner), tri, w1, w2)


def _sb_layer(h, norm_g, w_qkv, q_g, k_g, w_o, j, batch, seq, mlp_w1, mlp_w2, mlp_first, mlp_count):
    q_gain = q_g * np.float32(SB_HEAD_DIM ** -0.5 * LOG2E)
    qkv = _qkv_proj(h, norm_g, w_qkv, j, q_gain, k_g, tm=1024, tn=1024)
    o, w1b, w2b = _sb_attention(qkv, batch, seq, mlp_w1, mlp_w2, mlp_first, mlp_count,
                                tq=256, hp=4)
    h = _matmul_residual(o, w_o, j, h, tm=512, tn=w_o.shape[2])
    return h, w1b, w2b


def _gm_layer(h, norm_g, w_in, v_g, w_s, b_s, w_o, j):
    uv = _norm_matmul(h, norm_g, w_in, j, out_dtype=BF16, act="gelu", tm=1024, tn=1024)
    return _gm_gate(uv, v_g, w_s, b_s, h, w_o, j, tm=256)


def _ssd_layer(h, norm_g, w_in, w_in_f32, conv_w, conv_b, dt_bias, a_log, d_skip, ssd_norm_g, w_o, j,
               batch, seq, mlp_w1, mlp_w2, mlp_first, mlp_count):
    heads = d_skip.shape[0]
    main = w_in.shape[2] - heads
    hp = -(-heads // LANE) * LANE
    w_dt = jnp.pad(w_in_f32[j, :, main:], ((0, 0), (0, hp - heads))).astype(BF16)[None]
    zxbc = _norm_matmul(h, norm_g, w_in, j, out_dtype=BF16, tm=1024, tn=1024, n=main)
    dt_raw = _norm_matmul(h, norm_g, w_dt, 0, out_dtype=F32, tm=1024, tn=hp)
    y, w1b, w2b = _ssd_scan(zxbc, dt_raw, conv_w, conv_b, dt_bias, a_log, d_skip, ssd_norm_g, batch, seq,
                            mlp_w1, mlp_w2, mlp_first, mlp_count)
    return _matmul_residual(y, w_o, j, h, tm=1024, tn=512), w1b, w2b


def kernel(x, norm_mix_g, norm_mlp_g, sb_w_qkv, sb_q_norm_g, sb_k_norm_g, sb_w_o, gm_w_in, gm_v_norm_g, gm_w_s, gm_b_s, gm_w_o, ssd_w_in, ssd_conv_w, ssd_conv_b, ssd_dt_bias, ssd_a_log, ssd_d, ssd_norm_g, ssd_w_o, mlp_w_in, mlp_w_out):
    batch, seq, d = x.shape
    depth = norm_mix_g.shape[0]
    sb_w_qkv_b, sb_w_o_b = sb_w_qkv.astype(BF16), sb_w_o.astype(BF16)
    gm_w_in_b, gm_w_o_b = gm_w_in.astype(BF16), gm_w_o.astype(BF16)
    ssd_w_in_b, ssd_w_o_b = ssd_w_in.astype(BF16), ssd_w_o.astype(BF16)
    h = x.reshape(batch * seq, d)
    hosts = [i for i in range(depth) if i % N_MIXERS != 1]
    for i in range(depth):
        kind, j = i % N_MIXERS, i // N_MIXERS
        if i in hosts:
            mlp_first = i
            mlp_count = min([k for k in hosts if k > i] + [depth]) - i
        if kind == 0:
            h, mlp_w_in_b, mlp_w_out_b = _sb_layer(
                h, norm_mix_g[i], sb_w_qkv_b, sb_q_norm_g[j], sb_k_norm_g[j], sb_w_o_b, j, batch, seq,
                mlp_w_in, mlp_w_out, mlp_first, mlp_count)
        elif kind == 1:
            h = _gm_layer(h, norm_mix_g[i], gm_w_in_b, gm_v_norm_g[j], gm_w_s[j], gm_b_s[j], gm_w_o_b, j)
        else:
            h, mlp_w_in_b, mlp_w_out_b = _ssd_layer(
                h, norm_mix_g[i], ssd_w_in_b, ssd_w_in, ssd_conv_w[j], ssd_conv_b[j],
                ssd_dt_bias[j], ssd_a_log[j], ssd_d[j], ssd_norm_g[j], ssd_w_o_b, j, batch, seq,
                mlp_w_in, mlp_w_out, mlp_first, mlp_count)
        h = _mlp(h, norm_mlp_g[i], mlp_w_in_b, mlp_w_out_b, i - mlp_first, tm=512, th=1024)
    return h.reshape(batch, seq, d)
```

```python
import functools

import jax
import jax.numpy as jnp
import numpy as np
from jax import lax
from jax.experimental import pallas as pl
from jax.experimental.pallas import tpu as pltpu

F32 = jnp.float32
BF16 = jnp.bfloat16

EPS = 1e-6
LOG2E = float(np.log2(np.e))
LANE = 128
SUBLANE = 8
N_MIXERS = 3

SB_HEAD_DIM = 128
GM_CHUNK = 128
GM_GROUP_DIM = 128
SSD_HEAD_DIM = 64
SSD_GROUPS = 8
SSD_STATE = 128
SSD_CONV = 4
SSD_CHUNK = 128

VMEM_LIMIT_BYTES = 48 * 1024 * 1024
ATTN_VMEM_LIMIT_BYTES = 56 * 1024 * 1024
ROW_CHUNKS = 4
UNROLL = 4
CAST_SPLIT = 16
Q_TILES = 4


def _cparams(*sem):
    return pltpu.CompilerParams(dimension_semantics=sem, vmem_limit_bytes=VMEM_LIMIT_BYTES)


def _rms(x, g):
    ms = jnp.mean(x * x, axis=-1, keepdims=True)
    return x * lax.rsqrt(ms + EPS) * g


def _softplus(x):
    return jnp.maximum(x, 0.0) + jnp.log1p(jnp.exp(-jnp.abs(x)))


def _silu(x):
    return x * jax.nn.sigmoid(x)


def _split3(x):
    hi = x.astype(BF16)
    r = x - hi.astype(F32)
    mid = r.astype(BF16)
    lo = (r - mid.astype(F32)).astype(BF16)
    return hi, mid, lo


def _dot(a, b):
    return jnp.dot(a, b, preferred_element_type=F32)


def _dot_nt(a, b):
    return lax.dot_general(a, b, (((1,), (1,)), ((), ())), preferred_element_type=F32)


def _dot3_rhs(m, x):
    hi, mid, lo = _split3(x)
    return _dot(m, hi) + _dot(m, mid) + _dot(m, lo)


def _norm_mm_kernel(x_ref, g_ref, w_ref, o_ref, xn_ref, *, act):
    @pl.when(pl.program_id(1) == 0)
    def _():
        xn_ref[...] = _rms(x_ref[...], g_ref[...]).astype(BF16)

    w = w_ref[...]
    chunk = xn_ref.shape[0] // ROW_CHUNKS
    for r in range(ROW_CHUNKS):
        rows = slice(r * chunk, (r + 1) * chunk)
        acc = _dot(xn_ref[rows, :], w)
        if act == "gelu":
            acc = 0.5 * acc * (1.0 + lax.erf(acc * np.float32(np.sqrt(0.5))))
        o_ref[rows, :] = acc.astype(o_ref.dtype)


def _norm_matmul(x, g, w, layer, *, out_dtype, act=None, tm, tn, n=None):
    t, d = x.shape
    n = w.shape[2] if n is None else n
    return pl.pallas_call(
        functools.partial(_norm_mm_kernel, act=act),
        grid=(t // tm, n // tn),
        in_specs=[
            pl.BlockSpec((tm, d), lambda i, j: (i, 0)),
            pl.BlockSpec((1, d), lambda i, j: (0, 0)),
            pl.BlockSpec((None, d, tn), lambda i, j: (layer, 0, j)),
        ],
        out_specs=pl.BlockSpec((tm, tn), lambda i, j: (i, j)),
        out_shape=jax.ShapeDtypeStruct((t, n), out_dtype),
        scratch_shapes=[pltpu.VMEM((tm, d), BF16)],
        compiler_params=_cparams("parallel", "arbitrary"),
        name="norm_matmul",
    )(x, g.reshape(1, d), w)


def _qkv_kernel(x_ref, g_ref, w_ref, hg_ref, o_ref, xn_ref, *, tiles_per_part):
    j = pl.program_id(1)

    @pl.when(j == 0)
    def _():
        xn_ref[...] = _rms(x_ref[...], g_ref[...]).astype(BF16)

    is_v = j >= 2 * tiles_per_part
    gain = hg_ref[...]
    w = w_ref[...].astype(BF16)
    half = xn_ref.shape[0] // ROW_CHUNKS
    for r in range(ROW_CHUNKS):
        rows = slice(r * half, (r + 1) * half)
        acc = _dot(xn_ref[rows, :], w)
        for c in range(acc.shape[1] // SB_HEAD_DIM):
            cols = slice(c * SB_HEAD_DIM, (c + 1) * SB_HEAD_DIM)
            blk = acc[:, cols]
            ms = jnp.mean(blk * blk, axis=-1, keepdims=True)
            factor = jnp.where(is_v, 1.0, lax.rsqrt(ms + EPS))
            o_ref[rows, cols] = (blk * factor * gain).astype(o_ref.dtype)


def _qkv_proj(x, g, w, layer, q_g, k_g, *, tm, tn):
    t, d = x.shape
    n = w.shape[2]
    tiles_per_part = n // 3 // tn
    head_gains = jnp.stack([q_g, k_g, jnp.ones_like(q_g)]).reshape(3, 1, SB_HEAD_DIM)
    return pl.pallas_call(
        functools.partial(_qkv_kernel, tiles_per_part=tiles_per_part),
        grid=(t // tm, n // tn),
        in_specs=[
            pl.BlockSpec((tm, d), lambda i, j: (i, 0)),
            pl.BlockSpec((1, d), lambda i, j: (0, 0)),
            pl.BlockSpec((None, d, tn), lambda i, j: (layer, 0, j)),
            pl.BlockSpec((None, 1, SB_HEAD_DIM), lambda i, j: (j // tiles_per_part, 0, 0)),
        ],
        out_specs=pl.BlockSpec((tm, tn), lambda i, j: (i, j)),
        out_shape=jax.ShapeDtypeStruct((t, n), BF16),
        scratch_shapes=[pltpu.VMEM((tm, d), BF16)],
        compiler_params=_cparams("parallel", "arbitrary"),
        name="qkv_proj",
    )(x, g.reshape(1, d), w, head_gains)


def _cast_specs(w1, w2, first_layer, n_layers, n_steps, step_of):
    cast_blocks = n_layers * CAST_SPLIT
    assert cast_blocks <= n_steps
    blk = lambda *ids: jnp.minimum(step_of(*ids), cast_blocks - 1)
    in_specs, out_specs, out_shapes = [], [], []
    for wgt in (w1, w2):
        _, r, c = wgt.shape
        block = (None, r // CAST_SPLIT, c)
        in_specs.append(pl.BlockSpec(
            block, lambda *ids: (first_layer + blk(*ids) // CAST_SPLIT, blk(*ids) % CAST_SPLIT, 0)))
        out_specs.append(pl.BlockSpec(
            block, lambda *ids: (blk(*ids) // CAST_SPLIT, blk(*ids) % CAST_SPLIT, 0)))
        out_shapes.append(jax.ShapeDtypeStruct((n_layers, r, c), BF16))
    return in_specs, out_specs, out_shapes, cast_blocks


def _cast_side_job(step, cast_blocks, w1_ref, w2_ref, w1b_ref, w2b_ref):
    @pl.when(step < cast_blocks)
    def _():
        w1b_ref[...] = w1_ref[...].astype(BF16)
        w2b_ref[...] = w2_ref[...].astype(BF16)


def _sb_attn_kernel(q_ref, k_ref, v_ref, u_ref, w1_ref, w2_ref, o_ref, w1b_ref, w2b_ref,
                    lb_ref, l1m_ref, *, tq, hp, cast_blocks):
    step = (pl.program_id(0) * pl.num_programs(1) + pl.program_id(1)) * pl.num_programs(2) \
        + pl.program_id(2)
    _cast_side_job(step, cast_blocks, w1_ref, w2_ref, w1b_ref, w2b_ref)

    upper = u_ref[...]
    for sub in range(Q_TILES):
        _sb_attn_q_tile(q_ref, k_ref, v_ref, upper, o_ref, lb_ref, l1m_ref,
                        pl.program_id(2) * Q_TILES + sub, sub * tq, tq=tq, hp=hp)


def _sb_attn_q_tile(q_ref, k_ref, v_ref, upper, o_ref, lb_ref, l1m_ref, qi, row0, *, tq, hp):
    d = SB_HEAD_DIM
    parity = (row0 // tq) % 2
    q_rows = slice(row0, row0 + tq)

    def logits_stage(j, slot, diagonal):
        start = pl.multiple_of(j * tq, tq)
        if diagonal:
            row = lax.broadcasted_iota(jnp.int32, (tq, tq), 0)
            col = lax.broadcasted_iota(jnp.int32, (tq, tq), 1)
            causal = col < row
        for h in range(hp):
            z = _dot_nt(q_ref[q_rows, h * d:(h + 1) * d], k_ref[pl.ds(start, tq), h * d:(h + 1) * d])
            lse = jnp.log(1.0 + jnp.exp2(-jnp.abs(z))) * LOG2E
            log_beta = jnp.minimum(z, 0.0) - lse
            log_1m = log_beta - z
            if diagonal:
                log_1m = jnp.where(causal, log_1m, 0.0)
                log_beta = jnp.where(causal, log_beta, -jnp.inf)
            lb_ref[slot, h] = log_beta
            l1m_ref[slot, h] = log_1m.astype(BF16)

    def weights_stage(j, slot, carries, accs):
        start = pl.multiple_of(j * tq, tq)
        new_carries, new_accs = [], []
        for h in range(hp):
            l1m = l1m_ref[slot, h]
            suffix = _dot(l1m, upper)
            a = jnp.exp2(lb_ref[slot, h] + suffix + carries[h])
            vb = v_ref[pl.ds(start, tq), h * d:(h + 1) * d]
            new_accs.append(accs[h] + _dot(a.astype(BF16), vb))
            new_carries.append(carries[h] + suffix[:, 0:1] + l1m[:, 0:1].astype(F32))
        return tuple(new_carries), tuple(new_accs)

    carries = tuple(jnp.zeros((tq, 1), F32) for _ in range(hp))
    accs = tuple(jnp.zeros((tq, d), F32) for _ in range(hp))

    def run_tiles(j, count, first_slot, c):
        for u in range(count):
            logits_stage(j - u - 1, (first_slot + u + 1) % 2, False)
            c = weights_stage(j - u, (first_slot + u) % 2, c[0], c[1])
        return c

    carry = (carries, accs)
    logits_stage(qi, parity, True)
    if parity == 1:
        carry = run_tiles(qi, 1, 1, carry)
    done = parity
    block = 2
    while block < UNROLL:
        j = qi - done
        carry = lax.cond((qi // block) % 2 == 1,
                         functools.partial(run_tiles, j, block, 0), lambda c: c, carry)
        done = done + ((qi // block) % 2) * block
        block *= 2
    top = qi - qi % UNROLL

    def body(step, c):
        return run_tiles(top - UNROLL * step, UNROLL, 0, c)

    carry = lax.fori_loop(0, top // UNROLL, body, carry)
    carries, accs = weights_stage(0, 0, carry[0], carry[1])
    for h in range(hp):
        o_ref[q_rows, h * d:(h + 1) * d] = accs[h].astype(o_ref.dtype)


def _sb_attention(qkv, batch, seq, w1, w2, first_layer, n_layers, *, tq, hp):
    t, n = qkv.shape
    heads = n // (3 * SB_HEAD_DIM)
    hg = heads // hp
    nq = seq // tq
    w = hp * SB_HEAD_DIM
    s_idx = np.arange(tq)[:, None]
    j_idx = np.arange(tq)[None, :]
    upper = jnp.asarray((s_idx > j_idx).astype(np.float32), dtype=BF16)

    nqs = nq // Q_TILES
    cast_in, cast_out, cast_shapes, cast_blocks = _cast_specs(
        w1, w2, first_layer, n_layers, batch * hg * nqs, lambda b, h, i: (b * hg + h) * nqs + i)
    return pl.pallas_call(
        functools.partial(_sb_attn_kernel, tq=tq, hp=hp, cast_blocks=cast_blocks),
        grid=(batch, hg, nqs),
        in_specs=[
            pl.BlockSpec((Q_TILES * tq, w), lambda b, h, i: (b * nqs + i, h)),
            pl.BlockSpec((seq, w), lambda b, h, i: (b, hg + h)),
            pl.BlockSpec((seq, w), lambda b, h, i: (b, 2 * hg + h)),
            pl.BlockSpec((tq, tq), lambda b, h, i: (0, 0)),
        ] + cast_in,
        out_specs=[pl.BlockSpec((Q_TILES * tq, w), lambda b, h, i: (b * nqs + i, h))] + cast_out,
        out_shape=[jax.ShapeDtypeStruct((t, heads * SB_HEAD_DIM), BF16)] + cast_shapes,
        scratch_shapes=[pltpu.VMEM((2, hp, tq, tq), F32), pltpu.VMEM((2, hp, tq, tq), BF16)],
        compiler_params=pltpu.CompilerParams(
            dimension_semantics=("arbitrary", "arbitrary", "arbitrary"),
            vmem_limit_bytes=ATTN_VMEM_LIMIT_BYTES),
        name="sb_attention",
    )(qkv, qkv, qkv, upper, w1, w2)


def _mm_res_kernel(a_ref, w_ref, r_ref, o_ref):
    o_ref[...] = r_ref[...] + _dot(a_ref[...], w_ref[...])


def _matmul_residual(a, w, layer, res, *, tm, tn):
    t, k = a.shape
    n = w.shape[2]
    return pl.pallas_call(
        _mm_res_kernel,
        grid=(t // tm, n // tn),
        in_specs=[
            pl.BlockSpec((tm, k), lambda i, j: (i, 0)),
            pl.BlockSpec((None, k, tn), lambda i, j: (layer, 0, j)),
            pl.BlockSpec((tm, tn), lambda i, j: (i, j)),
        ],
        out_specs=pl.BlockSpec((tm, tn), lambda i, j: (i, j)),
        out_shape=jax.ShapeDtypeStruct((t, n), F32),
        compiler_params=_cparams("parallel", "arbitrary"),
        name="matmul_residual",
    )(a, w, res)


def _mlp_kernel(x_ref, g_ref, w1_ref, w2_ref, o_ref, xn_ref):
    @pl.when(pl.program_id(1) == 0)
    def _():
        x = x_ref[...]
        xn_ref[...] = _rms(x, g_ref[...]).astype(BF16)
        o_ref[...] = x

    h = _dot(xn_ref[...], w1_ref[...])
    h = jnp.square(jnp.maximum(h, 0.0))
    o_ref[...] += _dot(h.astype(BF16), w2_ref[...])


def _mlp(x, g, w1, w2, layer, *, tm, th):
    t, d = x.shape
    hidden = w1.shape[2]
    return pl.pallas_call(
        _mlp_kernel,
        grid=(t // tm, hidden // th),
        in_specs=[
            pl.BlockSpec((tm, d), lambda i, k: (i, 0)),
            pl.BlockSpec((1, d), lambda i, k: (0, 0)),
            pl.BlockSpec((None, d, th), lambda i, k: (layer, 0, k)),
            pl.BlockSpec((None, th, d), lambda i, k: (layer, k, 0)),
        ],
        out_specs=pl.BlockSpec((tm, d), lambda i, k: (i, 0)),
        out_shape=jax.ShapeDtypeStruct((t, d), F32),
        scratch_shapes=[pltpu.VMEM((tm, d), BF16)],
        compiler_params=_cparams("parallel", "arbitrary"),
        name="relu2_mlp",
    )(x, g.reshape(1, d), w1, w2)


def _gm_gate_kernel(u_ref, v_ref, vg_ref, ws_ref, bst_ref, x_ref, wo_ref, o_ref, vn_ref, y_ref):
    tm = u_ref.shape[0]
    groups = ws_ref.shape[0]
    vn_ref[...] = _rms(v_ref[...].astype(F32), vg_ref[...]).astype(BF16)
    row = lax.broadcasted_iota(jnp.int32, (GM_CHUNK, GM_CHUNK), 0)
    col = lax.broadcasted_iota(jnp.int32, (GM_CHUNK, GM_CHUNK), 1)
    causal = col <= row
    for g in range(groups):
        cols = slice(g * GM_GROUP_DIM, (g + 1) * GM_GROUP_DIM)
        w = jnp.where(causal, ws_ref[g], 0.0).astype(BF16)
        bias = bst_ref[:, g:g + 1]
        for c in range(tm // GM_CHUNK):
            rows = slice(c * GM_CHUNK, (c + 1) * GM_CHUNK)
            mixed = _dot(w, vn_ref[rows, cols]) + bias
            y_ref[rows, cols] = (u_ref[rows, cols].astype(F32) * mixed).astype(BF16)
    o_ref[...] = x_ref[...] + _dot(y_ref[...], wo_ref[...])


def _gm_gate(uv, v_g, w_s, b_s, x, w_o, layer, *, tm):
    t, d = x.shape
    width = uv.shape[1] // 2
    groups = w_s.shape[0]
    return pl.pallas_call(
        _gm_gate_kernel,
        grid=(t // tm,),
        in_specs=[
            pl.BlockSpec((tm, width), lambda i: (i, 0)),
            pl.BlockSpec((tm, width), lambda i: (i, 1)),
            pl.BlockSpec((1, width), lambda i: (0, 0)),
            pl.BlockSpec((groups, GM_CHUNK, GM_CHUNK), lambda i: (0, 0, 0)),
            pl.BlockSpec((GM_CHUNK, groups), lambda i: (0, 0)),
            pl.BlockSpec((tm, d), lambda i: (i, 0)),
            pl.BlockSpec((None, width, d), lambda i: (layer, 0, 0)),
        ],
        out_specs=pl.BlockSpec((tm, d), lambda i: (i, 0)),
        out_shape=jax.ShapeDtypeStruct((t, d), F32),
        scratch_shapes=[pltpu.VMEM((tm, width), BF16), pltpu.VMEM((tm, width), BF16)],
        compiler_params=_cparams("parallel"),
        name="gmlp_gate",
    )(uv, uv, v_g.reshape(1, width), w_s, b_s.T, x, w_o)


def _causal_conv_silu(ext_ref, cols, w, b):
    rows = ext_ref.shape[0] - SUBLANE
    acc = None
    for s in range(SSD_CONV):
        wk = w[SSD_CONV - 1 - s:SSD_CONV - s, :]
        term = ext_ref[pl.ds(SUBLANE - s, rows), cols] * wk
        acc = term if acc is None else acc + term
    return _silu(acc + b)


def _ssd_kernel(z_ref, xi_ref, bm_ref, cm_ref, dt_ref,
                wx_ref, wb_ref, wc_ref, bx_ref, bb_ref, bc_ref,
                dtb_ref, alog_ref, e_ref, dskip_ref, ng_ref, tri_ref, w1_ref, w2_ref,
                o_ref, w1b_ref, w2b_ref,
                state_ref, tx_ref, tb_ref, tc_ref, *, cast_blocks):
    _cast_side_job(pl.program_id(0) * pl.num_programs(1) + pl.program_id(1), cast_blocks,
                   w1_ref, w2_ref, w1b_ref, w2b_ref)
    L = SSD_CHUNK
    gw = state_ref.shape[2]
    hpg = gw // SSD_HEAD_DIM
    n = SSD_STATE

    top = slice(0, SUBLANE)
    cur = slice(SUBLANE, SUBLANE + L)

    @pl.when(pl.program_id(1) == 0)
    def _():
        state_ref[...] = jnp.zeros(state_ref.shape, F32)
        tx_ref[top, :] = jnp.zeros((SUBLANE, tx_ref.shape[1]), F32)
        tb_ref[top, :] = jnp.zeros((SUBLANE, tb_ref.shape[1]), F32)
        tc_ref[top, :] = jnp.zeros((SUBLANE, tc_ref.shape[1]), F32)

    tx_ref[cur, :] = xi_ref[...].astype(F32)
    tb_ref[cur, :] = bm_ref[...].astype(F32)
    tc_ref[cur, :] = cm_ref[...].astype(F32)

    dt = _softplus(dt_ref[...] + dtb_ref[...])
    a = -jnp.exp(alog_ref[...])
    a_cum = _dot3_rhs(tri_ref[...], dt * a)
    a_cum_t = a_cum.T
    dt_parts = _split3(dt)
    ac_parts = _split3(a_cum)

    def expand_heads(parts, sel):
        return _dot(parts[0], sel) + _dot(parts[1], sel) + _dot(parts[2], sel)

    row = lax.broadcasted_iota(jnp.int32, (L, L), 0)
    col = lax.broadcasted_iota(jnp.int32, (L, L), 1)
    tri = col <= row
    lane = lax.broadcasted_iota(jnp.int32, (L, 2 * SSD_HEAD_DIM), 1)
    first = lane < SSD_HEAD_DIM

    for g in range(state_ref.shape[0]):
        gx = slice(g * gw, (g + 1) * gw)
        gn = slice(g * n, (g + 1) * n)
        xi = _causal_conv_silu(tx_ref, gx, wx_ref[:, gx], bx_ref[:, gx])
        bm = _causal_conv_silu(tb_ref, gn, wb_ref[:, gn], bb_ref[:, gn])
        cm = _causal_conv_silu(tc_ref, gn, wc_ref[:, gn], bc_ref[:, gn])

        expand = e_ref[:, gx]
        dt_x = expand_heads(dt_parts, expand)
        ac_x = expand_heads(ac_parts, expand)
        last_x = ac_x[L - 1:L, :]
        xs = xi * dt_x
        xsd = xs * jnp.exp(last_x - ac_x)

        cm_b = cm.astype(BF16)
        bm_b = bm.astype(BF16)
        xs_b = xs.astype(BF16)
        state = state_ref[g]
        y = _dot(cm_b, state.astype(BF16)) * jnp.exp(ac_x)
        state_ref[g] = jnp.exp(last_x) * state + _dot(bm.T.astype(BF16), xsd.astype(BF16))

        cb = _dot_nt(cm_b, bm_b)
        y_diag = []
        for pair in range(hpg // 2):
            outs = []
            for r in (2 * pair, 2 * pair + 1):
                head = g * hpg + r
                a_col = ac_x[:, r * SSD_HEAD_DIM:r * SSD_HEAD_DIM + 1]
                a_row = a_cum_t[head:head + 1, :]
                decay = jnp.exp(jnp.where(tri, a_col - a_row, -jnp.inf))
                m = (cb * decay).astype(BF16)
                outs.append(_dot(m, xs_b[:, pair * 2 * SSD_HEAD_DIM:(pair + 1) * 2 * SSD_HEAD_DIM]))
            y_diag.append(jnp.where(first, outs[0], outs[1]))
        y = y + jnp.concatenate(y_diag, axis=1)
        y = y + dskip_ref[:, gx] * xi
        y = y * _silu(z_ref[:, gx].astype(F32))
        o_ref[:, gx] = _rms(y, ng_ref[:, gx]).astype(o_ref.dtype)

    last = slice(L, L + SUBLANE)
    tx_ref[top, :] = tx_ref[last, :]
    tb_ref[top, :] = tb_ref[last, :]
    tc_ref[top, :] = tc_ref[last, :]


def _ssd_scan(zxbc, dt_raw, conv_w, conv_b, dt_bias, a_log, d_skip, norm_g, batch, seq,
              w1, w2, first_layer, n_layers):
    t = zxbc.shape[0]
    heads = d_skip.shape[0]
    inner = heads * SSD_HEAD_DIM
    gw = inner // SSD_GROUPS
    gn = SSD_GROUPS * SSD_STATE
    nc = seq // SSD_CHUNK
    hp = dt_raw.shape[1]
    L = SSD_CHUNK

    pad = hp - heads
    dtb = jnp.pad(dt_bias, (0, pad)).reshape(1, hp)
    alog = jnp.pad(a_log, (0, pad)).reshape(1, hp)
    e = np.zeros((hp, inner), np.float32)
    for h in range(heads):
        e[h, h * SSD_HEAD_DIM:(h + 1) * SSD_HEAD_DIM] = 1.0
    e = jnp.asarray(e, dtype=BF16)
    tri = jnp.asarray(np.tril(np.ones((L, L), np.float32)), dtype=BF16)
    dskip_x = jnp.repeat(d_skip, SSD_HEAD_DIM).reshape(1, inner)
    conv_b2 = conv_b.reshape(1, -1)

    row_blk = lambda b, c: b * nc + c
    const = lambda b, c: (0, 0)
    in_specs = [
        pl.BlockSpec((L, inner), lambda b, c: (row_blk(b, c), 0)),
        pl.BlockSpec((L, inner), lambda b, c: (row_blk(b, c), 1)),
        pl.BlockSpec((L, gn), lambda b, c: (row_blk(b, c), 2 * inner // gn)),
        pl.BlockSpec((L, gn), lambda b, c: (row_blk(b, c), 2 * inner // gn + 1)),
        pl.BlockSpec((L, hp), lambda b, c: (row_blk(b, c), 0)),
        pl.BlockSpec((SSD_CONV, inner), const),
        pl.BlockSpec((SSD_CONV, gn), lambda b, c: (0, inner // gn)),
        pl.BlockSpec((SSD_CONV, gn), lambda b, c: (0, inner // gn + 1)),
        pl.BlockSpec((1, inner), const),
        pl.BlockSpec((1, gn), lambda b, c: (0, inner // gn)),
        pl.BlockSpec((1, gn), lambda b, c: (0, inner // gn + 1)),
        pl.BlockSpec((1, hp), const),
        pl.BlockSpec((1, hp), const),
        pl.BlockSpec((hp, inner), const),
        pl.BlockSpec((1, inner), const),
        pl.BlockSpec((1, inner), const),
        pl.BlockSpec((L, L), const),
    ]
    cast_in, cast_out, cast_shapes, cast_blocks = _cast_specs(
        w1, w2, first_layer, n_layers, batch * nc, row_blk)
    return pl.pallas_call(
        functools.partial(_ssd_kernel, cast_blocks=cast_blocks),
        grid=(batch, nc),
        in_specs=in_specs + cast_in,
        out_specs=[pl.BlockSpec((L, inner), lambda b, c: (row_blk(b, c), 0))] + cast_out,
        out_shape=[jax.ShapeDtypeStruct((t, inner), BF16)] + cast_shapes,
        scratch_shapes=[
            pltpu.VMEM((SSD_GROUPS, SSD_STATE, gw), F32),
            pltpu.VMEM((SUBLANE + L, inner), F32),
            pltpu.VMEM((SUBLANE + L, gn), F32),
            pltpu.VMEM((SUBLANE + L, gn), F32),
        ],
        compiler_params=_cparams("arbitrary", "arbitrary"),
        name="ssd_scan",
    )(zxbc, zxbc, zxbc, zxbc, dt_raw,
      conv_w, conv_w, conv_w, conv_b2, conv_b2, conv_b2,
      dtb, alog, e, dskip_x, norm_g.reshape(1, inner), tri, w1, w2)


def _sb_layer(h, norm_g, w_qkv, q_g, k_g, w_o, j, batch, seq, mlp_w1, mlp_w2, mlp_first, mlp_count):
    q_gain = q_g * np.float32(SB_HEAD_DIM ** -0.5 * LOG2E)
    qkv = _qkv_proj(h, norm_g, w_qkv, j, q_gain, k_g, tm=1024, tn=1024)
    o, w1b, w2b = _sb_attention(qkv, batch, seq, mlp_w1, mlp_w2, mlp_first, mlp_count,
                                tq=256, hp=4)
    h = _matmul_residual(o, w_o, j, h, tm=512, tn=w_o.shape[2])
    return h, w1b, w2b


def _gm_layer(h, norm_g, w_in, v_g, w_s, b_s, w_o, j):
    uv = _norm_matmul(h, norm_g, w_in, j, out_dtype=BF16, act="gelu", tm=1024, tn=1024)
    return _gm_gate(uv, v_g, w_s, b_s, h, w_o, j, tm=256)


def _ssd_layer(h, norm_g, w_in, w_in_f32, conv_w, conv_b, dt_bias, a_log, d_skip, ssd_norm_g, w_o, j,
               batch, seq, mlp_w1, mlp_w2, mlp_first, mlp_count):
    heads = d_skip.shape[0]
    main = w_in.shape[2] - heads
    hp = -(-heads // LANE) * LANE
    w_dt = jnp.pad(w_in_f32[j, :, main:], ((0, 0), (0, hp - heads))).astype(BF16)[None]
    zxbc = _norm_matmul(h, norm_g, w_in, j, out_dtype=BF16, tm=1024, tn=1024, n=main)
    dt_raw = _norm_matmul(h, norm_g, w_dt, 0, out_dtype=F32, tm=1024, tn=hp)
    y, w1b, w2b = _ssd_scan(zxbc, dt_raw, conv_w, conv_b, dt_bias, a_log, d_skip, ssd_norm_g, batch, seq,
                            mlp_w1, mlp_w2, mlp_first, mlp_count)
    return _matmul_residual(y, w_o, j, h, tm=1024, tn=512), w1b, w2b


def kernel(x, norm_mix_g, norm_mlp_g, sb_w_qkv, sb_q_norm_g, sb_k_norm_g, sb_w_o, gm_w_in, gm_v_norm_g, gm_w_s, gm_b_s, gm_w_o, ssd_w_in, ssd_conv_w, ssd_conv_b, ssd_dt_bias, ssd_a_log, ssd_d, ssd_norm_g, ssd_w_o, mlp_w_in, mlp_w_out):
    batch, seq, d = x.shape
    depth = norm_mix_g.shape[0]
    sb_w_qkv_b, sb_w_o_b = sb_w_qkv, sb_w_o.astype(BF16)
    gm_w_in_b, gm_w_o_b = gm_w_in.astype(BF16), gm_w_o.astype(BF16)
    ssd_w_in_b, ssd_w_o_b = ssd_w_in.astype(BF16), ssd_w_o.astype(BF16)
    h = x.reshape(batch * seq, d)
    hosts = [i for i in range(depth) if i % N_MIXERS != 1]
    for i in range(depth):
        kind, j = i % N_MIXERS, i // N_MIXERS
        if i in hosts:
            mlp_first = i
            mlp_count = min([k for k in hosts if k > i] + [depth]) - i
        if kind == 0:
            h, mlp_w_in_b, mlp_w_out_b = _sb_layer(
                h, norm_mix_g[i], sb_w_qkv_b, sb_q_norm_g[j], sb_k_norm_g[j], sb_w_o_b, j, batch, seq,
                mlp_w_in, mlp_w_out, mlp_first, mlp_count)
        elif kind == 1:
            h = _gm_layer(h, norm_mix_g[i], gm_w_in_b, gm_v_norm_g[j], gm_w_s[j], gm_b_s[j], gm_w_o_b, j)
        else:
            h, mlp_w_in_b, mlp_w_out_b = _ssd_layer(
                h, norm_mix_g[i], ssd_w_in_b, ssd_w_in, ssd_conv_w[j], ssd_conv_b[j],
                ssd_dt_bias[j], ssd_a_log[j], ssd_d[j], ssd_norm_g[j], ssd_w_o_b, j, batch, seq,
                mlp_w_in, mlp_w_out, mlp_first, mlp_count)
        h = _mlp(h, norm_mlp_g[i], mlp_w_in_b, mlp_w_out_b, i - mlp_first, tm=512, th=1024)
    return h.reshape(batch, seq, d)
```
